```python
import math
import jax, jax.numpy as jnp
from jax import lax
import numpy as np

D_MODEL = 1024
BATCH = 32
SEQ = 2048
DEPTH = 2

MLA_HEADS = 8
MLA_NOPE = 64
MLA_ROPE = 32
MLA_QK = MLA_NOPE + MLA_ROPE
MLA_V = 64
Q_LORA = 384
KV_LORA = 256
ROPE_THETA = 10000.0
Q_BLOCK = 128
MLA_WIDTH = MLA_HEADS * MLA_V

HG_HEADS = 4
HG_DK = 128
HG_DV = 128
HG_CHUNK = 32
HG_KW = HG_HEADS * HG_DK
HG_WIDTH = HG_HEADS * HG_DV

D_FF = -(-8 * D_MODEL // (3 * 256)) * 256

NORM_EPS = 1e-6

IN_SPLITS = (Q_LORA, KV_LORA, MLA_ROPE, HG_KW, HG_KW, HG_WIDTH, HG_WIDTH, D_MODEL, D_MODEL)
D_IN = sum(IN_SPLITS)

kernel_name = "hybrid_mla_hgrn2_gated_merge"


def rms_norm(x, g):
    xf = x.astype(jnp.float32)
    y = xf * lax.rsqrt(jnp.mean(xf * xf, axis=-1, keepdims=True) + NORM_EPS)
    return (y * g.astype(jnp.float32)).astype(x.dtype)


def rope_tables(seq):
    pos = jnp.arange(seq, dtype=jnp.float32)
    inv_freq = 1.0 / (ROPE_THETA ** (jnp.arange(0, MLA_ROPE, 2, dtype=jnp.float32) / MLA_ROPE))
    ang = pos[:, None] * inv_freq[None, :]
    return jnp.cos(ang), jnp.sin(ang)


def apply_rope(x, cos, sin):
    half = x.shape[-1] // 2
    xf = x.astype(jnp.float32)
    x1, x2 = xf[..., :half], xf[..., half:]
    return jnp.concatenate([x1 * cos - x2 * sin, x1 * sin + x2 * cos], axis=-1).astype(x.dtype)


def mla_mixer(c_q, c_kv, k_pe, norm_cq, w_uq, norm_ckv, w_ukv, q_norm, k_norm, cos, sin):
    B, T, _ = c_q.shape
    q = (rms_norm(c_q, norm_cq) @ w_uq).reshape(B, T, MLA_HEADS, MLA_QK)
    kv = (rms_norm(c_kv, norm_ckv) @ w_ukv).reshape(B, T, MLA_HEADS, MLA_NOPE + MLA_V)
    k_nope, v = kv[..., :MLA_NOPE], kv[..., MLA_NOPE:]
    k = jnp.concatenate([k_nope, jnp.broadcast_to(k_pe[:, :, None, :], (B, T, MLA_HEADS, MLA_ROPE))], axis=-1)
    q = rms_norm(q, q_norm)
    k = rms_norm(k, k_norm)
    c, s = cos[:, None, :], sin[:, None, :]
    q = jnp.concatenate([q[..., :MLA_NOPE], apply_rope(q[..., MLA_NOPE:], c, s)], axis=-1)
    k = jnp.concatenate([k[..., :MLA_NOPE], apply_rope(k[..., MLA_NOPE:], c, s)], axis=-1)
    scale = MLA_QK ** -0.5
    outs = []
    for blk in range(T // Q_BLOCK):
        s0 = blk * Q_BLOCK
        s1 = s0 + Q_BLOCK
        qb, kb, vb = q[:, s0:s1], k[:, :s1], v[:, :s1]
        logits = jnp.einsum('bqhd,bkhd->bhqk', qb, kb).astype(jnp.float32) * scale
        mask = (s0 + jnp.arange(Q_BLOCK))[:, None] >= jnp.arange(s1)[None, :]
        logits = jnp.where(mask, logits, -jnp.inf)
        p = jax.nn.softmax(logits, axis=-1).astype(v.dtype)
        outs.append(jnp.einsum('bhqk,bkhd->bqhd', p, vb))
    return jnp.concatenate(outs, axis=1).reshape(B, T, MLA_WIDTH)


def hgrn2_mixer(q, f_logit, i, g, lb, out_norm):
    B, T, _ = q.shape
    N, C = T // HG_CHUNK, HG_CHUNK
    lbf = lb.astype(jnp.float32)
    log_f = jnp.logaddexp(jnp.log(lbf), jnp.log1p(-lbf) + jax.nn.log_sigmoid(f_logit.astype(jnp.float32)))
    k = -jnp.expm1(log_f)

    def to_chunks(a, d):
        return a.astype(jnp.float32).reshape(B, N, C, HG_HEADS, d).transpose(1, 0, 3, 2, 4)

    qc = to_chunks(q, HG_DK)
    kc = to_chunks(k, HG_DK)
    vc = to_chunks(i, HG_DV)
    bc = jnp.cumsum(to_chunks(log_f, HG_DK), axis=3)
    causal = jnp.arange(C)[:, None] >= jnp.arange(C)[None, :]

    def step(S, inp):
        qx, kx, vx, bx = inp
        b_last = bx[:, :, -1:, :]
        o_inter = jnp.einsum('bhck,bhkv->bhcv', qx * jnp.exp(bx), S)
        diff = bx[:, :, :, None, :] - bx[:, :, None, :, :]
        decay = jnp.exp(jnp.where(causal[:, :, None], diff, -jnp.inf))
        A = jnp.einsum('bhtk,bhsk,bhtsk->bhts', qx, kx, decay)
        o_intra = jnp.einsum('bhts,bhsv->bhtv', A, vx)
        S_new = jnp.exp(b_last[:, :, 0, :])[..., None] * S + jnp.einsum('bhsk,bhsv->bhkv', kx * jnp.exp(b_last - bx), vx)
        return S_new, o_inter + o_intra

    S0 = jnp.zeros((B, HG_HEADS, HG_DK, HG_DV), jnp.float32)
    _, o = lax.scan(step, S0, (qc, kc, vc, bc))
    o = o.transpose(1, 0, 3, 2, 4).reshape(B, T, HG_HEADS, HG_DV).astype(q.dtype)
    o = rms_norm(o, out_norm) * jax.nn.silu(g.reshape(B, T, HG_HEADS, HG_DV))
    return o.reshape(B, T, HG_WIDTH)


def setup_inputs(seed: int = 0) -> dict:
    key = jax.random.key(seed)
    ks = jax.random.split(key, 20)

    def w(k, shape, fan_in):
        return jax.random.normal(k, shape, jnp.float32) * fan_in ** -0.5

    def gain(k, shape):
        return 1.0 + 0.02 * jax.random.normal(k, shape, jnp.float32)

    L = DEPTH
    return {
        "x": jax.random.normal(ks[0], (BATCH, SEQ, D_MODEL), jnp.float32),
        "norm_mix": gain(ks[1], (L, D_MODEL)),
        "w_in": w(ks[2], (L, D_MODEL, D_IN), D_MODEL),
        "mla_norm_cq": gain(ks[3], (L, Q_LORA)),
        "mla_w_uq": w(ks[4], (L, Q_LORA, MLA_HEADS * MLA_QK), Q_LORA),
        "mla_norm_ckv": gain(ks[5], (L, KV_LORA)),
        "mla_w_ukv": w(ks[6], (L, KV_LORA, MLA_HEADS * (MLA_NOPE + MLA_V)), KV_LORA),
        "mla_q_norm": gain(ks[7], (L, MLA_QK)),
        "mla_k_norm": gain(ks[8], (L, MLA_QK)),
        "hg_lb_logits": 0.5 * jax.random.normal(ks[9], (L, HG_KW), jnp.float32),
        "hg_out_norm": gain(ks[10], (L, HG_DV)),
        "w_proj_a": w(ks[11], (L, MLA_WIDTH, D_MODEL), MLA_WIDTH),
        "w_proj_b": w(ks[12], (L, HG_WIDTH, D_MODEL), HG_WIDTH),
        "w_out": w(ks[13], (L, D_MODEL, D_MODEL), D_MODEL),
        "norm_ffn": gain(ks[14], (L, D_MODEL)),
        "w_gate": w(ks[15], (L, D_MODEL, D_FF), D_MODEL),
        "w_up": w(ks[16], (L, D_MODEL, D_FF), D_MODEL),
        "w_down": w(ks[17], (L, D_FF, D_MODEL), D_FF),
    }


def reference(x, norm_mix, w_in, mla_norm_cq, mla_w_uq, mla_norm_ckv, mla_w_ukv, mla_q_norm, mla_k_norm,
              hg_lb_logits, hg_out_norm, w_proj_a, w_proj_b, w_out, norm_ffn, w_gate, w_up, w_down):
    T = x.shape[1]
    cos, sin = rope_tables(T)
    p = jax.nn.softmax(hg_lb_logits.astype(jnp.float32), axis=0)
    lower_bounds = jnp.maximum(jnp.cumsum(p, axis=0) - p[0:1], 0.0)
    offsets = [int(o) for o in np.cumsum(IN_SPLITS)[:-1]]

    for l in range(DEPTH):
        h = rms_norm(x, norm_mix[l])
        z = h @ w_in[l]
        c_q, c_kv, k_pe, hq, hf, hi, hg, g_a, g_b = jnp.split(z, offsets, axis=-1)
        y_a = mla_mixer(c_q, c_kv, k_pe, mla_norm_cq[l], mla_w_uq[l], mla_norm_ckv[l], mla_w_ukv[l],
                        mla_q_norm[l], mla_k_norm[l], cos, sin) @ w_proj_a[l]
        y_b = hgrn2_mixer(hq, hf, hi, hg, lower_bounds[l], hg_out_norm[l]) @ w_proj_b[l]
        y = jax.nn.sigmoid(g_a) * y_a + jax.nn.sigmoid(g_b) * y_b
        x = x + y @ w_out[l]
        h = rms_norm(x, norm_ffn[l])
        x = x + (jax.nn.silu(h @ w_gate[l]) * (h @ w_up[l])) @ w_down[l]
    return x
```

```python
import functools

import jax
import jax.numpy as jnp
from jax import lax
from jax.experimental import pallas as pl
from jax.experimental.pallas import tpu as pltpu

D_MODEL = 1024
MLA_HEADS = 8
MLA_NOPE = 64
MLA_ROPE = 32
MLA_QK = MLA_NOPE + MLA_ROPE
MLA_V = 64
Q_LORA = 384
KV_LORA = 256
ROPE_THETA = 10000.0
HG_HEADS = 4
HG_DK = 128
HG_DV = 128
HG_KW = HG_HEADS * HG_DK
HG_WIDTH = HG_HEADS * HG_DV
NORM_EPS = 1e-6
IN_SPLITS = (Q_LORA, KV_LORA, MLA_ROPE, HG_KW, HG_KW, HG_WIDTH, HG_WIDTH, D_MODEL, D_MODEL)

LANES = 128
HEAD_PAD = LANES
HG_CHUNK = 32
HG_SAFE_DECAY = 60.0
VMEM_LIMIT = 56 * 1024 * 1024

BF16 = jnp.bfloat16
F32 = jnp.float32
NT = (((1,), (1,)), ((), ()))
TN = (((0,), (0,)), ((), ()))


def _rms(x, g):
    return x * lax.rsqrt(jnp.mean(x * x, axis=-1, keepdims=True) + NORM_EPS) * g


def _dot(a, b):
    return jnp.dot(a, b, preferred_element_type=F32)


def _const_spec(shape):
    nd = len(shape)
    return pl.BlockSpec(shape, lambda *_: (0,) * nd, pipeline_mode=pl.Buffered(1))


def _params(sem):
    return pltpu.CompilerParams(dimension_semantics=sem, vmem_limit_bytes=VMEM_LIMIT)


def _mla_prep_kernel(x_ref, gmix_ref, wmla_ref, gcq_ref, wuq_ref, gckv_ref, wukv_ref,
                     cq_ref, sq_ref, ck_ref, sk_ref, q_ref, k_ref, v_ref):
    h = _rms(x_ref[...], gmix_ref[...]).astype(BF16)
    zc = _dot(h, wmla_ref[...])
    c_q = zc[:, :Q_LORA]
    c_kv = zc[:, Q_LORA:Q_LORA + KV_LORA]
    kpe_a = zc[:, 640:768]
    kpe_b = zc[:, 768:896]
    q2 = _dot(_rms(c_q, gcq_ref[...]).astype(BF16), wuq_ref[...])
    kv2 = _dot(_rms(c_kv, gckv_ref[...]).astype(BF16), wukv_ref[...])
    cq, sq, ck, sk = cq_ref[...], sq_ref[...], ck_ref[...], sk_ref[...]
    inv_qk = 1.0 / MLA_QK
    for hd in range(MLA_HEADS):
        lo = hd * HEAD_PAD
        qa = q2[:, lo:lo + HEAD_PAD]
        qb = q2[:, MLA_HEADS * HEAD_PAD + lo:MLA_HEADS * HEAD_PAD + lo + HEAD_PAD]
        rq = lax.rsqrt(jnp.sum(qa * qa, axis=-1, keepdims=True) * inv_qk + NORM_EPS)
        q_ref[:, lo:lo + HEAD_PAD] = ((qa * cq + qb * sq) * rq).astype(BF16)
        ka = kv2[:, lo:lo + HEAD_PAD] + kpe_a
        rk = lax.rsqrt(jnp.sum(ka * ka, axis=-1, keepdims=True) * inv_qk + NORM_EPS)
        k_ref[:, lo:lo + HEAD_PAD] = ((ka * ck + kpe_b * sk) * rk).astype(BF16)
    v_ref[...] = kv2[:, MLA_HEADS * HEAD_PAD:].astype(BF16)


def _mla_prep(x2, gmix, wmla, gcq, wuq, gckv, wukv, tabs, seq, tm):
    n = x2.shape[0]
    nt = seq // tm
    tab_spec = pl.BlockSpec((tm, LANES), lambda i: (i % nt, 0))
    return pl.pallas_call(
        _mla_prep_kernel,
        grid=(n // tm,),
        in_specs=[
            pl.BlockSpec((tm, D_MODEL), lambda i: (i, 0)),
            _const_spec(gmix.shape), _const_spec(wmla.shape), _const_spec(gcq.shape),
            _const_spec(wuq.shape), _const_spec(gckv.shape), _const_spec(wukv.shape),
            tab_spec, tab_spec, tab_spec, tab_spec,
        ],
        out_specs=[
            pl.BlockSpec((tm, MLA_HEADS * HEAD_PAD), lambda i: (i, 0)),
            pl.BlockSpec((tm, MLA_HEADS * HEAD_PAD), lambda i: (i, 0)),
            pl.BlockSpec((tm, MLA_HEADS * MLA_V), lambda i: (i, 0)),
        ],
        out_shape=[
            jax.ShapeDtypeStruct((n, MLA_HEADS * HEAD_PAD), BF16),
            jax.ShapeDtypeStruct((n, MLA_HEADS * HEAD_PAD), BF16),
            jax.ShapeDtypeStruct((n, MLA_HEADS * MLA_V), BF16),
        ],
        compiler_params=_params(("parallel",)),
        name="mla_prep",
    )(x2, gmix, wmla, gcq, wuq, gckv, wukv, *tabs)


def _attn_kernel(q_ref, k_ref, v_ref, o_ref, *, tq):
    i = pl.program_id(2)
    row = lax.broadcasted_iota(jnp.int32, (tq, tq), 0)
    col = lax.broadcasted_iota(jnp.int32, (tq, tq), 1)
    causal = row >= col
    outs = []
    for hh in range(2):
        q = q_ref[:, hh * HEAD_PAD:(hh + 1) * HEAD_PAD]

        def step(start, carry, masked, q=q, hh=hh):
            m, l, acc = carry
            kb = k_ref[pl.ds(start, tq), hh * HEAD_PAD:(hh + 1) * HEAD_PAD]
            s = lax.dot_general(q, kb, NT, preferred_element_type=F32)
            if masked:
                s = jnp.where(causal, s, -jnp.inf)
            m_new = jnp.maximum(m, jnp.max(s, axis=-1, keepdims=True))
            alpha = jnp.exp(m - m_new)
            p = jnp.exp(s - m_new)
            l = alpha * l + jnp.sum(p, axis=-1, keepdims=True)
            acc = alpha * acc + _dot(p.astype(BF16), v_ref[pl.ds(start, tq), :])
            return m_new, l, acc

        init = (jnp.full((tq, 1), -jnp.inf, F32), jnp.zeros((tq, 1), F32), jnp.zeros((tq, LANES), F32))
        carry = lax.fori_loop(
            0, i, lambda j, c: step(pl.multiple_of(j * tq, tq), c, False), init)
        m, l, acc = step(pl.multiple_of(i * tq, tq), carry, True)
        outs.append(acc / l)
    lane = lax.broadcasted_iota(jnp.int32, (tq, LANES), 1)
    o_ref[...] = jnp.where(lane < MLA_V, outs[0], outs[1]).astype(BF16)


def _mla_attn(q, k, v, batch, seq, tq):
    nq = seq // tq
    return pl.pallas_call(
        functools.partial(_attn_kernel, tq=tq),
        grid=(batch, MLA_HEADS // 2, nq),
        in_specs=[
            pl.BlockSpec((tq, 2 * HEAD_PAD), lambda b, hp, i: (b * nq + i, hp)),
            pl.BlockSpec((seq, 2 * HEAD_PAD), lambda b, hp, i: (b, hp)),
            pl.BlockSpec((seq, 2 * MLA_V), lambda b, hp, i: (b, hp)),
        ],
        out_specs=pl.BlockSpec((tq, 2 * MLA_V), lambda b, hp, i: (b * nq + i, hp)),
        out_shape=jax.ShapeDtypeStruct((batch * seq, MLA_HEADS * MLA_V), BF16),
        compiler_params=_params(("parallel", "parallel", "arbitrary")),
        name="mla_attn",
    )(q, k, v)


def _split3(x):
    a = x.astype(BF16)
    r = x - a.astype(F32)
    b = r.astype(BF16)
    c = (r - b.astype(F32)).astype(BF16)
    return a, b, c


def _hgrn_kernel(x_ref, gmix_ref, whg_ref, loglb_ref, log1mlb_ref, gout_ref, tri_ref, o_ref,
                 st_ref, stbak_ref, q_scr, k_scr, v_scr, b_scr, o_scr, *, tm):
    nc = tm // HG_CHUNK
    C = HG_CHUNK

    @pl.when(pl.program_id(1) == 0)
    def _():
        st_ref[...] = jnp.zeros_like(st_ref)

    h = _rms(x_ref[...], gmix_ref[...]).astype(BF16)
    z = _dot(h, whg_ref[...])
    zq = z[:, :HG_KW]
    zf = z[:, HG_KW:2 * HG_KW]
    zi = z[:, 2 * HG_KW:2 * HG_KW + HG_WIDTH]
    zg = z[:, 2 * HG_KW + HG_WIDTH:]
    log_sig = jnp.minimum(zf, 0.0) - jnp.log1p(jnp.exp(-jnp.abs(zf)))
    a = loglb_ref[...]
    c = log1mlb_ref[...] + log_sig
    log_f = jnp.maximum(a, c) + jnp.log1p(jnp.exp(-jnp.abs(a - c)))
    kk = 1.0 - jnp.exp(log_f)
    tri = tri_ref[...]
    l1, l2, l3 = _split3(log_f)
    b = _dot(tri, l1) + _dot(tri, l2) + _dot(tri, l3)
    b3 = b.reshape(nc, C, HG_KW)
    b_last = b3[:, C - 1:C, :]
    q_scr[...] = zq
    k_scr[...] = kk
    v_scr[...] = zi
    b_scr[...] = b
    stbak_ref[...] = st_ref[...]
    unsafe = jnp.min(b) < -HG_SAFE_DECAY

    qe = (zq * jnp.exp(b)).astype(BF16)
    ke = (kk * jnp.exp(jnp.minimum(-b, HG_SAFE_DECAY + 20.0))).astype(BF16)
    kl = (kk.reshape(nc, C, HG_KW) * jnp.exp(b_last - b3)).reshape(tm, HG_KW).astype(BF16)
    e_last = jnp.exp(b_last)
    vb = zi.astype(BF16)
    row = lax.broadcasted_iota(jnp.int32, (C, C), 0)
    col = lax.broadcasted_iota(jnp.int32, (C, C), 1)
    causal = row >= col
    for ci in range(nc):
        r0 = ci * C
        for hd in range(HG_HEADS):
            lo = hd * HG_DK
            qe_c = qe[r0:r0 + C, lo:lo + HG_DK]
            v_c = vb[r0:r0 + C, lo:lo + HG_DV]
            att = lax.dot_general(qe_c, ke[r0:r0 + C, lo:lo + HG_DK], NT, preferred_element_type=F32)
            att = jnp.where(causal, att, 0.0).astype(BF16)
            st = st_ref[hd]
            o_c = _dot(att, v_c) + lax.dot_general(qe_c, st.astype(BF16), NT, preferred_element_type=F32)
            o_scr[r0:r0 + C, lo:lo + HG_DV] = o_c
            upd = lax.dot_general(v_c, kl[r0:r0 + C, lo:lo + HG_DK], TN, preferred_element_type=F32)
            st_ref[hd] = st * e_last[ci, :, lo:lo + HG_DK] + upd

    @pl.when(unsafe)
    def _():
        st_ref[...] = stbak_ref[...]
        t_idx = lax.broadcasted_iota(jnp.int32, (C, 1), 0)
        for hd in range(HG_HEADS):
            lo = hd * HG_DK

            def chunk_body(ci, _, lo=lo, hd=hd):
                r0 = pl.multiple_of(ci * C, C)
                q_c = q_scr[pl.ds(r0, C), lo:lo + HG_DK]
                k_c = k_scr[pl.ds(r0, C), lo:lo + HG_DK]
                v_c = v_scr[pl.ds(r0, C), lo:lo + HG_DV]
                b_c = b_scr[pl.ds(r0, C), lo:lo + HG_DK]
                bl = b_c[C - 1:C, :]

                def s_body(s, o_acc):
                    pick = t_idx == s
                    ks = jnp.sum(jnp.where(pick, k_c, 0.0), axis=0, keepdims=True)
                    bs = jnp.sum(jnp.where(pick, b_c, 0.0), axis=0, keepdims=True)
                    vs = jnp.sum(jnp.where(pick, v_c, 0.0), axis=0, keepdims=True)
                    dec = jnp.exp(jnp.minimum(b_c - bs, 0.0))
                    w = jnp.sum(q_c * ks * dec, axis=-1, keepdims=True)
                    w = jnp.where(t_idx >= s, w, 0.0)
                    return o_acc + w * vs

                o_intra = lax.fori_loop(0, C, s_body, jnp.zeros((C, HG_DV), F32))
                st = st_ref[hd]
                qe_c = (q_c * jnp.exp(b_c)).astype(BF16)
                o_inter = lax.dot_general(qe_c, st.astype(BF16), NT, preferred_element_type=F32)
                o_scr[pl.ds(r0, C), lo:lo + HG_DV] = o_intra + o_inter
                kl_c = (k_c * jnp.exp(bl - b_c)).astype(BF16)
                upd = lax.dot_general(v_c.astype(BF16), kl_c, TN, preferred_element_type=F32)
                st_ref[hd] = st * jnp.exp(bl) + upd
                return 0

            lax.fori_loop(0, nc, chunk_body, 0)

    o = o_scr[...]
    gate = zg * jax.nn.sigmoid(zg)
    gout = gout_ref[...]
    for hd in range(HG_HEADS):
        lo = hd * HG_DV
        o_ref[:, lo:lo + HG_DV] = (_rms(o[:, lo:lo + HG_DV], gout) * gate[:, lo:lo + HG_DV]).astype(BF16)


def _hgrn(x2, gmix, whg, loglb, log1mlb, gout, tri, batch, seq, tm):
    nt = seq // tm
    return pl.pallas_call(
        functools.partial(_hgrn_kernel, tm=tm),
        grid=(batch, nt),
        in_specs=[
            pl.BlockSpec((tm, D_MODEL), lambda b, i: (b * nt + i, 0)),
            _const_spec(gmix.shape), _const_spec(whg.shape), _const_spec(loglb.shape),
            _const_spec(log1mlb.shape), _const_spec(gout.shape), _const_spec(tri.shape),
        ],
        out_specs=pl.BlockSpec((tm, HG_WIDTH), lambda b, i: (b * nt + i, 0)),
        out_shape=jax.ShapeDtypeStruct((batch * seq, HG_WIDTH), BF16),
        scratch_shapes=[
            pltpu.VMEM((HG_HEADS, HG_DV, HG_DK), F32),
            pltpu.VMEM((HG_HEADS, HG_DV, HG_DK), F32),
            pltpu.VMEM((tm, HG_KW), F32),
            pltpu.VMEM((tm, HG_KW), F32),
            pltpu.VMEM((tm, HG_WIDTH), F32),
            pltpu.VMEM((tm, HG_KW), F32),
            pltpu.VMEM((tm, HG_WIDTH), F32),
        ],
        compiler_params=_params(("parallel", "arbitrary")),
        name="hgrn",
    )(x2, gmix, whg, loglb, log1mlb, gout, tri)


def _merge_kernel(x_ref, ya_ref, yb_ref, gmix_ref, wg_ref, wpa_ref, wpb_ref, wout_ref, o_ref):
    x = x_ref[...]
    h = _rms(x, gmix_ref[...]).astype(BF16)
    g = _dot(h, wg_ref[...])
    y_a = _dot(ya_ref[...], wpa_ref[...])
    y_b = _dot(yb_ref[...], wpb_ref[...])
    y = jax.nn.sigmoid(g[:, :D_MODEL]) * y_a + jax.nn.sigmoid(g[:, D_MODEL:]) * y_b
    o_ref[...] = x + _dot(y.astype(BF16), wout_ref[...])


def _merge(x2, ya, yb, gmix, wg, wpa, wpb, wout, tm):
    n = x2.shape[0]
    return pl.pallas_call(
        _merge_kernel,
        grid=(n // tm,),
        in_specs=[
            pl.BlockSpec((tm, D_MODEL), lambda i: (i, 0)),
            pl.BlockSpec((tm, ya.shape[1]), lambda i: (i, 0)),
            pl.BlockSpec((tm, yb.shape[1]), lambda i: (i, 0)),
            _const_spec(gmix.shape), _const_spec(wg.shape), _const_spec(wpa.shape),
            _const_spec(wpb.shape), _const_spec(wout.shape),
        ],
        out_specs=pl.BlockSpec((tm, D_MODEL), lambda i: (i, 0)),
        out_shape=jax.ShapeDtypeStruct((n, D_MODEL), F32),
        compiler_params=_params(("parallel",)),
        name="merge",
    )(x2, ya, yb, gmix, wg, wpa, wpb, wout)


def _ffn_kernel(x_ref, gffn_ref, wgate_ref, wup_ref, wdown_ref, o_ref, *, n_split):
    x = x_ref[...]
    h = _rms(x, gffn_ref[...]).astype(BF16)
    d_ff = wgate_ref.shape[1]
    w = d_ff // n_split
    acc = x
    for j in range(n_split):
        gate = _dot(h, wgate_ref[:, j * w:(j + 1) * w])
        up = _dot(h, wup_ref[:, j * w:(j + 1) * w])
        act = (gate * jax.nn.sigmoid(gate) * up).astype(BF16)
        acc = acc + _dot(act, wdown_ref[j * w:(j + 1) * w, :])
    o_ref[...] = acc


def _ffn(x2, gffn, wgate, wup, wdown, tm, n_split):
    n = x2.shape[0]
    return pl.pallas_call(
        functools.partial(_ffn_kernel, n_split=n_split),
        grid=(n // tm,),
        in_specs=[
            pl.BlockSpec((tm, D_MODEL), lambda i: (i, 0)),
            _const_spec(gffn.shape), _const_spec(wgate.shape), _const_spec(wup.shape),
            _const_spec(wdown.shape),
        ],
        out_specs=pl.BlockSpec((tm, D_MODEL), lambda i: (i, 0)),
        out_shape=jax.ShapeDtypeStruct((n, D_MODEL), F32),
        compiler_params=_params(("parallel",)),
        name="ffn",
    )(x2, gffn, wgate, wup, wdown)


def _rope_tables(seq, gain, scale):
    pos = jnp.arange(seq, dtype=F32)
    inv_freq = 1.0 / (ROPE_THETA ** (jnp.arange(0, MLA_ROPE, 2, dtype=F32) / MLA_ROPE))
    ang = pos[:, None] * inv_freq[None, :]
    cos, sin = jnp.cos(ang), jnp.sin(ang)
    half = MLA_ROPE // 2
    g_nope, g_rope = gain[:MLA_NOPE], gain[MLA_NOPE:]
    pad = jnp.zeros((seq, HEAD_PAD - MLA_QK), F32)
    ctab = jnp.concatenate([
        jnp.broadcast_to(g_nope[None, :], (seq, MLA_NOPE)),
        jnp.concatenate([cos, cos], axis=1) * g_rope[None, :], pad], axis=1)
    g_partner = jnp.concatenate([g_rope[half:], g_rope[:half]])
    stab = jnp.concatenate([
        jnp.zeros((seq, MLA_NOPE), F32),
        jnp.concatenate([-sin, sin], axis=1) * g_partner[None, :], pad], axis=1)
    return ctab * scale, stab * scale


def _layer_params(l, seq, tm_hg, w_in, mla_w_uq, mla_w_ukv, mla_q_norm, mla_k_norm, lower_bounds):
    o = [0]
    for s in IN_SPLITS:
        o.append(o[-1] + s)
    wi = w_in[l]
    half = MLA_ROPE // 2
    z64 = jnp.zeros((D_MODEL, MLA_NOPE), F32)
    z32 = jnp.zeros((D_MODEL, HEAD_PAD - MLA_QK), F32)
    w_kpe = wi[:, o[2]:o[3]]
    w_kpe_sw = jnp.concatenate([w_kpe[:, half:], w_kpe[:, :half]], axis=1)
    wmla = jnp.concatenate([wi[:, o[0]:o[2]], z64, w_kpe, z32, z64, w_kpe_sw, z32], axis=1).astype(BF16)
    whg = wi[:, o[3]:o[7]].astype(BF16)
    wgates = wi[:, o[7]:o[9]].astype(BF16)

    uq = mla_w_uq[l].reshape(Q_LORA, MLA_HEADS, MLA_QK)
    uq_nope, uq_rope = uq[..., :MLA_NOPE], uq[..., MLA_NOPE:]
    uq_rope_sw = jnp.concatenate([uq_rope[..., half:], uq_rope[..., :half]], axis=-1)
    zq32 = jnp.zeros((Q_LORA, MLA_HEADS, HEAD_PAD - MLA_QK), F32)
    zq64 = jnp.zeros((Q_LORA, MLA_HEADS, MLA_NOPE), F32)
    wuq_a = jnp.concatenate([uq_nope, uq_rope, zq32], axis=-1).reshape(Q_LORA, MLA_HEADS * HEAD_PAD)
    wuq_b = jnp.concatenate([zq64, uq_rope_sw, zq32], axis=-1).reshape(Q_LORA, MLA_HEADS * HEAD_PAD)
    wuq = jnp.concatenate([wuq_a, wuq_b], axis=1).astype(BF16)

    ukv = mla_w_ukv[l].reshape(KV_LORA, MLA_HEADS, MLA_NOPE + MLA_V)
    zk64 = jnp.zeros((KV_LORA, MLA_HEADS, HEAD_PAD - MLA_NOPE), F32)
    wuk = jnp.concatenate([ukv[..., :MLA_NOPE], zk64], axis=-1).reshape(KV_LORA, MLA_HEADS * HEAD_PAD)
    wuv = ukv[..., MLA_NOPE:].reshape(KV_LORA, MLA_HEADS * MLA_V)
    wukv = jnp.concatenate([wuk, wuv], axis=1).astype(BF16)

    cq, sq = _rope_tables(seq, mla_q_norm[l], MLA_QK ** -0.5)
    ck, sk = _rope_tables(seq, mla_k_norm[l], 1.0)

    lb = lower_bounds[l][None, :]
    r = jnp.arange(tm_hg)
    tri = ((r[:, None] // HG_CHUNK == r[None, :] // HG_CHUNK) & (r[:, None] >= r[None, :])).astype(BF16)
    return dict(wmla=wmla, whg=whg, wgates=wgates, wuq=wuq, wukv=wukv, tabs=(cq, sq, ck, sk),
                loglb=jnp.log(lb), log1mlb=jnp.log1p(-lb), tri=tri)


def kernel(x, norm_mix, w_in, mla_norm_cq, mla_w_uq, mla_norm_ckv, mla_w_ukv, mla_q_norm, mla_k_norm,
           hg_lb_logits, hg_out_norm, w_proj_a, w_proj_b, w_out, norm_ffn, w_gate, w_up, w_down):
    batch, seq, d = x.shape
    assert d == D_MODEL
    depth = w_in.shape[0]
    tm = min(256, seq)
    tq = min(256, seq)
    tm_ffn = min(512, seq)
    assert seq % tm == 0 and seq % tm_ffn == 0 and tm % HG_CHUNK == 0

    p = jax.nn.softmax(hg_lb_logits.astype(F32), axis=0)
    lower_bounds = jnp.maximum(jnp.cumsum(p, axis=0) - p[0:1], 0.0)

    x2 = x.reshape(batch * seq, d)
    for l in range(depth):
        lp = _layer_params(l, seq, tm, w_in, mla_w_uq, mla_w_ukv, mla_q_norm, mla_k_norm, lower_bounds)
        gmix = norm_mix[l][None, :]
        q, k, v = _mla_prep(x2, gmix, lp["wmla"], mla_norm_cq[l][None, :], lp["wuq"],
                            mla_norm_ckv[l][None, :], lp["wukv"], lp["tabs"], seq, tm)
        y_a = _mla_attn(q, k, v, batch, seq, tq)
        y_b = _hgrn(x2, gmix, lp["whg"], lp["loglb"], lp["log1mlb"], hg_out_norm[l][None, :],
                    lp["tri"], batch, seq, tm)
        x2 = _merge(x2, y_a, y_b, gmix, lp["wgates"], w_proj_a[l].astype(BF16),
                    w_proj_b[l].astype(BF16), w_out[l].astype(BF16), tm_ffn)
        x2 = _ffn(x2, norm_ffn[l][None, :], w_gate[l].astype(BF16), w_up[l].astype(BF16),
                  w_down[l].astype(BF16), tm_ffn, 2)
    return x2.reshape(batch, seq, d)
```

```python
import functools

import jax
import jax.numpy as jnp
from jax import lax
from jax.experimental import pallas as pl
from jax.experimental.pallas import tpu as pltpu

D_MODEL = 1024
MLA_HEADS = 8
MLA_NOPE = 64
MLA_ROPE = 32
MLA_QK = MLA_NOPE + MLA_ROPE
MLA_V = 64
Q_LORA = 384
KV_LORA = 256
ROPE_THETA = 10000.0
HG_HEADS = 4
HG_DK = 128
HG_DV = 128
HG_KW = HG_HEADS * HG_DK
HG_WIDTH = HG_HEADS * HG_DV
NORM_EPS = 1e-6
IN_SPLITS = (Q_LORA, KV_LORA, MLA_ROPE, HG_KW, HG_KW, HG_WIDTH, HG_WIDTH, D_MODEL, D_MODEL)

LANES = 128
HEAD_PAD = LANES
HG_CHUNK = 32
HG_SAFE_DECAY = 60.0
LOG2_E = 1.4426950408889634
VMEM_LIMIT = 56 * 1024 * 1024

BF16 = jnp.bfloat16
F32 = jnp.float32
NT = (((1,), (1,)), ((), ()))
TN = (((0,), (0,)), ((), ()))


def _rms(x, g):
    return x * lax.rsqrt(jnp.mean(x * x, axis=-1, keepdims=True) + NORM_EPS) * g


def _dot(a, b):
    return jnp.dot(a, b, preferred_element_type=F32)


def _const_spec(shape):
    nd = len(shape)
    return pl.BlockSpec(shape, lambda *_: (0,) * nd, pipeline_mode=pl.Buffered(1))


def _params(sem):
    return pltpu.CompilerParams(dimension_semantics=sem, vmem_limit_bytes=VMEM_LIMIT)


def _mla_prep_kernel(x_ref, gmix_ref, wmla_ref, gcq_ref, wuq_ref, gckv_ref, wukv_ref,
                     cq_ref, sq_ref, ck_ref, sk_ref, q_ref, k_ref, vt_ref):
    h = _rms(x_ref[...], gmix_ref[...]).astype(BF16)
    zc = _dot(h, wmla_ref[...])
    c_q = zc[:, :Q_LORA]
    c_kv = zc[:, Q_LORA:Q_LORA + KV_LORA]
    kpe_a = zc[:, 640:768]
    kpe_b = zc[:, 768:896]
    q2 = _dot(_rms(c_q, gcq_ref[...]).astype(BF16), wuq_ref[...])
    kv2 = _dot(_rms(c_kv, gckv_ref[...]).astype(BF16), wukv_ref[...])
    cq, sq, ck, sk = cq_ref[...], sq_ref[...], ck_ref[...], sk_ref[...]
    inv_qk = 1.0 / MLA_QK
    for hd in range(MLA_HEADS):
        lo = hd * HEAD_PAD
        qa = q2[:, lo:lo + HEAD_PAD]
        qb = q2[:, MLA_HEADS * HEAD_PAD + lo:MLA_HEADS * HEAD_PAD + lo + HEAD_PAD]
        rq = lax.rsqrt(jnp.sum(qa * qa, axis=-1, keepdims=True) * inv_qk + NORM_EPS)
        q_ref[:, lo:lo + HEAD_PAD] = ((qa * cq + qb * sq) * rq).astype(BF16)
        ka = kv2[:, lo:lo + HEAD_PAD] + kpe_a
        rk = lax.rsqrt(jnp.sum(ka * ka, axis=-1, keepdims=True) * inv_qk + NORM_EPS)
        k_ref[:, lo:lo + HEAD_PAD] = ((ka * ck + kpe_b * sk) * rk).astype(BF16)
    v = kv2[:, MLA_HEADS * HEAD_PAD:]
    for hp in range(MLA_HEADS // 2):
        vt_ref[hp] = v[:, hp * LANES:(hp + 1) * LANES].T.astype(BF16)


def _mla_prep(x2, gmix, wmla, gcq, wuq, gckv, wukv, tabs, seq, tm):
    n = x2.shape[0]
    nt = seq // tm
    batch = n // seq
    tab_spec = pl.BlockSpec((tm, LANES), lambda i: (i % nt, 0))
    return pl.pallas_call(
        _mla_prep_kernel,
        grid=(n // tm,),
        in_specs=[
            pl.BlockSpec((tm, D_MODEL), lambda i: (i, 0)),
            _const_spec(gmix.shape), _const_spec(wmla.shape), _const_spec(gcq.shape),
            _const_spec(wuq.shape), _const_spec(gckv.shape), _const_spec(wukv.shape),
            tab_spec, tab_spec, tab_spec, tab_spec,
        ],
        out_specs=[
            pl.BlockSpec((tm, MLA_HEADS * HEAD_PAD), lambda i: (i, 0)),
            pl.BlockSpec((tm, MLA_HEADS * HEAD_PAD), lambda i: (i, 0)),
            pl.BlockSpec((None, MLA_HEADS // 2, None, LANES, tm), lambda i: (i // nt, 0, i % nt, 0, 0)),
        ],
        out_shape=[
            jax.ShapeDtypeStruct((n, MLA_HEADS * HEAD_PAD), BF16),
            jax.ShapeDtypeStruct((n, MLA_HEADS * HEAD_PAD), BF16),
            jax.ShapeDtypeStruct((batch, MLA_HEADS // 2, nt, LANES, tm), BF16),
        ],
        compiler_params=_params(("parallel",)),
        name="mla_prep",
    )(x2, gmix, wmla, gcq, wuq, gckv, wukv, *tabs)


def _attn_kernel(q_ref, k_ref, vt_ref, o_ref, *, tq):
    i = pl.program_id(2)
    kv_idx = lax.broadcasted_iota(jnp.int32, (tq, tq), 0)
    q_idx = lax.broadcasted_iota(jnp.int32, (tq, tq), 1)
    causal = q_idx >= kv_idx
    qs = [q_ref[:, hh * HEAD_PAD:(hh + 1) * HEAD_PAD] for hh in range(2)]

    def scores(j):
        start = pl.multiple_of(j * tq, tq)
        return tuple(
            lax.dot_general(k_ref[pl.ds(start, tq), hh * HEAD_PAD:(hh + 1) * HEAD_PAD], qs[hh], NT,
                            preferred_element_type=F32) for hh in range(2))

    def update(j, s_pair, carry, masked):
        vt = vt_ref[j]
        out = []
        for hh in range(2):
            m, l, acc = carry[hh]
            s = s_pair[hh]
            if masked:
                s = jnp.where(causal, s, -jnp.inf)
            m_new = jnp.maximum(m, jnp.max(s, axis=0, keepdims=True))
            alpha = jnp.exp2(m - m_new)
            p = jnp.exp2(s - m_new)
            l = alpha * l + jnp.sum(p, axis=0, keepdims=True)
            acc = alpha * acc + _dot(vt[hh * MLA_V:(hh + 1) * MLA_V, :], p.astype(BF16))
            out.append((m_new, l, acc))
        return tuple(out)

    init = tuple((jnp.full((1, tq), -jnp.inf, F32), jnp.zeros((1, tq), F32), jnp.zeros((MLA_V, tq), F32))
                 for _ in range(2))

    def body(j, c):
        s_next = scores(j + 1)
        return s_next, update(j, c[0], c[1], False)

    s_last, carry = lax.fori_loop(0, i, body, (scores(0), init))
    carry = update(i, s_last, carry, True)
    o_t = jnp.concatenate([acc / l for _, l, acc in carry], axis=0)
    o_ref[...] = o_t.T.astype(BF16)


def _mla_attn(q, k, vt, batch, seq, tq):
    nq = seq // tq
    return pl.pallas_call(
        functools.partial(_attn_kernel, tq=tq),
        grid=(batch, MLA_HEADS // 2, nq),
        in_specs=[
            pl.BlockSpec((tq, 2 * HEAD_PAD), lambda b, hp, i: (b * nq + i, hp)),
            pl.BlockSpec((seq, 2 * HEAD_PAD), lambda b, hp, i: (b, hp)),
            pl.BlockSpec((None, None, nq, LANES, tq), lambda b, hp, i: (b, hp, 0, 0, 0)),
        ],
        out_specs=pl.BlockSpec((tq, 2 * MLA_V), lambda b, hp, i: (b * nq + i, hp)),
        out_shape=jax.ShapeDtypeStruct((batch * seq, MLA_HEADS * MLA_V), BF16),
        compiler_params=_params(("parallel", "parallel", "arbitrary")),
        name="mla_attn",
    )(q, k, vt)


def _split3(x):
    a = x.astype(BF16)
    r = x - a.astype(F32)
    b = r.astype(BF16)
    c = (r - b.astype(F32)).astype(BF16)
    return a, b, c


def _hgrn_kernel(x_ref, gmix_ref, whg_ref, loglb_ref, log1mlb_ref, gout_ref, tri_ref, cmask_ref, o_ref,
                 st_ref, stbak_ref, q_scr, k_scr, v_scr, b_scr, o_scr, *, tm):
    nc = tm // HG_CHUNK
    C = HG_CHUNK

    @pl.when(pl.program_id(1) == 0)
    def _():
        st_ref[...] = jnp.zeros_like(st_ref)

    h = _rms(x_ref[...], gmix_ref[...]).astype(BF16)
    z = _dot(h, whg_ref[...])
    zq = z[:, :HG_KW]
    zf = z[:, HG_KW:2 * HG_KW]
    zi = z[:, 2 * HG_KW:2 * HG_KW + HG_WIDTH]
    zg = z[:, 2 * HG_KW + HG_WIDTH:]
    log_sig = jnp.minimum(zf, 0.0) - jnp.log1p(jnp.exp(-jnp.abs(zf)))
    a = loglb_ref[...]
    c = log1mlb_ref[...] + log_sig
    log_f = jnp.maximum(a, c) + jnp.log1p(jnp.exp(-jnp.abs(a - c)))
    kk = 1.0 - jnp.exp(log_f)
    tri = tri_ref[...]
    l1, l2, l3 = _split3(log_f)
    b = _dot(tri, l1) + _dot(tri, l2) + _dot(tri, l3)
    b3 = b.reshape(nc, C, HG_KW)
    b_last = b3[:, C - 1:C, :]
    q_scr[...] = zq
    k_scr[...] = kk
    v_scr[...] = zi
    b_scr[...] = b
    stbak_ref[...] = st_ref[...]
    unsafe = jnp.min(b) < -HG_SAFE_DECAY

    qe = (zq * jnp.exp(b)).astype(BF16)
    ke = (kk * jnp.exp(jnp.minimum(-b, HG_SAFE_DECAY + 20.0))).astype(BF16)
    kl = (kk.reshape(nc, C, HG_KW) * jnp.exp(b_last - b3)).reshape(tm, HG_KW).astype(BF16)
    e_last = jnp.exp(b_last)
    vb = zi.astype(BF16)
    vt = zi.T.astype(BF16)
    row = lax.broadcasted_iota(jnp.int32, (tm, tm), 0)
    col = lax.broadcasted_iota(jnp.int32, (tm, tm), 1)
    causal = (row >= col) & (row // C == col // C)
    cmask = cmask_ref[...]
    for hd in range(HG_HEADS):
        lo = hd * HG_DK
        qe_h = qe[:, lo:lo + HG_DK]
        att = lax.dot_general(qe_h, ke[:, lo:lo + HG_DK], NT, preferred_element_type=F32)
        att = jnp.where(causal, att, 0.0).astype(BF16)
        o_intra = _dot(att, vb[:, lo:lo + HG_DV])
        vt_h = vt[lo:lo + HG_DV, :]
        upd = _dot(jnp.concatenate([vt_h] * nc, axis=0) * cmask, kl[:, lo:lo + HG_DK])
        st = st_ref[hd]
        for ci in range(nc):
            r0 = ci * C
            o_inter = lax.dot_general(qe_h[r0:r0 + C], st.astype(BF16), NT, preferred_element_type=F32)
            o_scr[r0:r0 + C, lo:lo + HG_DV] = o_intra[r0:r0 + C] + o_inter
            st = st * e_last[ci, :, lo:lo + HG_DK] + upd[ci * HG_DV:(ci + 1) * HG_DV]
        st_ref[hd] = st

    @pl.when(unsafe)
    def _():
        st_ref[...] = stbak_ref[...]
        t_idx = lax.broadcasted_iota(jnp.int32, (C, 1), 0)
        for hd in range(HG_HEADS):
            lo = hd * HG_DK

            def chunk_body(ci, _, lo=lo, hd=hd):
                r0 = pl.multiple_of(ci * C, C)
                q_c = q_scr[pl.ds(r0, C), lo:lo + HG_DK]
                k_c = k_scr[pl.ds(r0, C), lo:lo + HG_DK]
                v_c = v_scr[pl.ds(r0, C), lo:lo + HG_DV]
                b_c = b_scr[pl.ds(r0, C), lo:lo + HG_DK]
                bl = b_c[C - 1:C, :]

                def s_body(s, o_acc):
                    pick = t_idx == s
                    ks = jnp.sum(jnp.where(pick, k_c, 0.0), axis=0, keepdims=True)
                    bs = jnp.sum(jnp.where(pick, b_c, 0.0), axis=0, keepdims=True)
                    vs = jnp.sum(jnp.where(pick, v_c, 0.0), axis=0, keepdims=True)
                    dec = jnp.exp(jnp.minimum(b_c - bs, 0.0))
                    w = jnp.sum(q_c * ks * dec, axis=-1, keepdims=True)
                    w = jnp.where(t_idx >= s, w, 0.0)
                    return o_acc + w * vs

                o_intra = lax.fori_loop(0, C, s_body, jnp.zeros((C, HG_DV), F32))
                st = st_ref[hd]
                qe_c = (q_c * jnp.exp(b_c)).astype(BF16)
                o_inter = lax.dot_general(qe_c, st.astype(BF16), NT, preferred_element_type=F32)
                o_scr[pl.ds(r0, C), lo:lo + HG_DV] = o_intra + o_inter
                kl_c = (k_c * jnp.exp(bl - b_c)).astype(BF16)
                upd = lax.dot_general(v_c.astype(BF16), kl_c, TN, preferred_element_type=F32)
                st_ref[hd] = st * jnp.exp(bl) + upd
                return 0

            lax.fori_loop(0, nc, chunk_body, 0)

    o = o_scr[...]
    gate = zg * jax.nn.sigmoid(zg)
    gout = gout_ref[...]
    for hd in range(HG_HEADS):
        lo = hd * HG_DV
        o_ref[:, lo:lo + HG_DV] = (_rms(o[:, lo:lo + HG_DV], gout) * gate[:, lo:lo + HG_DV]).astype(BF16)


def _hgrn(x2, gmix, whg, loglb, log1mlb, gout, tri, cmask, batch, seq, tm):
    nt = seq // tm
    return pl.pallas_call(
        functools.partial(_hgrn_kernel, tm=tm),
        grid=(batch, nt),
        in_specs=[
            pl.BlockSpec((tm, D_MODEL), lambda b, i: (b * nt + i, 0)),
            _const_spec(gmix.shape), _const_spec(whg.shape), _const_spec(loglb.shape),
            _const_spec(log1mlb.shape), _const_spec(gout.shape), _const_spec(tri.shape),
            _const_spec(cmask.shape),
        ],
        out_specs=pl.BlockSpec((tm, HG_WIDTH), lambda b, i: (b * nt + i, 0)),
        out_shape=jax.ShapeDtypeStruct((batch * seq, HG_WIDTH), BF16),
        scratch_shapes=[
            pltpu.VMEM((HG_HEADS, HG_DV, HG_DK), F32),
            pltpu.VMEM((HG_HEADS, HG_DV, HG_DK), F32),
            pltpu.VMEM((tm, HG_KW), F32),
            pltpu.VMEM((tm, HG_KW), F32),
            pltpu.VMEM((tm, HG_WIDTH), F32),
            pltpu.VMEM((tm, HG_KW), F32),
            pltpu.VMEM((tm, HG_WIDTH), F32),
        ],
        compiler_params=_params(("parallel", "arbitrary")),
        name="hgrn",
    )(x2, gmix, whg, loglb, log1mlb, gout, tri, cmask)


def _merge_kernel(x_ref, ya_ref, yb_ref, gmix_ref, wg_ref, wpa_ref, wpb_ref, wout_ref, o_ref):
    x = x_ref[...]
    h = _rms(x, gmix_ref[...]).astype(BF16)
    g = _dot(h, wg_ref[...])
    y_a = _dot(ya_ref[...], wpa_ref[...])
    y_b = _dot(yb_ref[...], wpb_ref[...])
    y = jax.nn.sigmoid(g[:, :D_MODEL]) * y_a + jax.nn.sigmoid(g[:, D_MODEL:]) * y_b
    o_ref[...] = x + _dot(y.astype(BF16), wout_ref[...])


def _merge(x2, ya, yb, gmix, wg, wpa, wpb, wout, tm):
    n = x2.shape[0]
    return pl.pallas_call(
        _merge_kernel,
        grid=(n // tm,),
        in_specs=[
            pl.BlockSpec((tm, D_MODEL), lambda i: (i, 0)),
            pl.BlockSpec((tm, ya.shape[1]), lambda i: (i, 0)),
            pl.BlockSpec((tm, yb.shape[1]), lambda i: (i, 0)),
            _const_spec(gmix.shape), _const_spec(wg.shape), _const_spec(wpa.shape),
            _const_spec(wpb.shape), _const_spec(wout.shape),
        ],
        out_specs=pl.BlockSpec((tm, D_MODEL), lambda i: (i, 0)),
        out_shape=jax.ShapeDtypeStruct((n, D_MODEL), F32),
        compiler_params=_params(("parallel",)),
        name="merge",
    )(x2, ya, yb, gmix, wg, wpa, wpb, wout)


def _ffn_kernel(x_ref, gffn_ref, wgate_ref, wup_ref, wdown_ref, o_ref, *, n_split):
    x = x_ref[...]
    h = _rms(x, gffn_ref[...]).astype(BF16)
    d_ff = wgate_ref.shape[1]
    w = d_ff // n_split
    acc = x
    for j in range(n_split):
        gate = _dot(h, wgate_ref[:, j * w:(j + 1) * w])
        up = _dot(h, wup_ref[:, j * w:(j + 1) * w])
        act = (gate * jax.nn.sigmoid(gate) * up).astype(BF16)
        acc = acc + _dot(act, wdown_ref[j * w:(j + 1) * w, :])
    o_ref[...] = acc


def _ffn(x2, gffn, wgate, wup, wdown, tm, n_split):
    n = x2.shape[0]
    return pl.pallas_call(
        functools.partial(_ffn_kernel, n_split=n_split),
        grid=(n // tm,),
        in_specs=[
            pl.BlockSpec((tm, D_MODEL), lambda i: (i, 0)),
            _const_spec(gffn.shape), _const_spec(wgate.shape), _const_spec(wup.shape),
            _const_spec(wdown.shape),
        ],
        out_specs=pl.BlockSpec((tm, D_MODEL), lambda i: (i, 0)),
        out_shape=jax.ShapeDtypeStruct((n, D_MODEL), F32),
        compiler_params=_params(("parallel",)),
        name="ffn",
    )(x2, gffn, wgate, wup, wdown)


def _rope_tables(seq, gain, scale):
    pos = jnp.arange(seq, dtype=F32)
    inv_freq = 1.0 / (ROPE_THETA ** (jnp.arange(0, MLA_ROPE, 2, dtype=F32) / MLA_ROPE))
    ang = pos[:, None] * inv_freq[None, :]
    cos, sin = jnp.cos(ang), jnp.sin(ang)
    half = MLA_ROPE // 2
    g_nope, g_rope = gain[:MLA_NOPE], gain[MLA_NOPE:]
    pad = jnp.zeros((seq, HEAD_PAD - MLA_QK), F32)
    ctab = jnp.concatenate([
        jnp.broadcast_to(g_nope[None, :], (seq, MLA_NOPE)),
        jnp.concatenate([cos, cos], axis=1) * g_rope[None, :], pad], axis=1)
    g_partner = jnp.concatenate([g_rope[half:], g_rope[:half]])
    stab = jnp.concatenate([
        jnp.zeros((seq, MLA_NOPE), F32),
        jnp.concatenate([-sin, sin], axis=1) * g_partner[None, :], pad], axis=1)
    return ctab * scale, stab * scale


def _layer_params(l, seq, tm_hg, w_in, mla_w_uq, mla_w_ukv, mla_q_norm, mla_k_norm, lower_bounds):
    o = [0]
    for s in IN_SPLITS:
        o.append(o[-1] + s)
    wi = w_in[l]
    half = MLA_ROPE // 2
    z64 = jnp.zeros((D_MODEL, MLA_NOPE), F32)
    z32 = jnp.zeros((D_MODEL, HEAD_PAD - MLA_QK), F32)
    w_kpe = wi[:, o[2]:o[3]]
    w_kpe_sw = jnp.concatenate([w_kpe[:, half:], w_kpe[:, :half]], axis=1)
    wmla = jnp.concatenate([wi[:, o[0]:o[2]], z64, w_kpe, z32, z64, w_kpe_sw, z32], axis=1).astype(BF16)
    whg = wi[:, o[3]:o[7]].astype(BF16)
    wgates = wi[:, o[7]:o[9]].astype(BF16)

    uq = mla_w_uq[l].reshape(Q_LORA, MLA_HEADS, MLA_QK)
    uq_nope, uq_rope = uq[..., :MLA_NOPE], uq[..., MLA_NOPE:]
    uq_rope_sw = jnp.concatenate([uq_rope[..., half:], uq_rope[..., :half]], axis=-1)
    zq32 = jnp.zeros((Q_LORA, MLA_HEADS, HEAD_PAD - MLA_QK), F32)
    zq64 = jnp.zeros((Q_LORA, MLA_HEADS, MLA_NOPE), F32)
    wuq_a = jnp.concatenate([uq_nope, uq_rope, zq32], axis=-1).reshape(Q_LORA, MLA_HEADS * HEAD_PAD)
    wuq_b = jnp.concatenate([zq64, uq_rope_sw, zq32], axis=-1).reshape(Q_LORA, MLA_HEADS * HEAD_PAD)
    wuq = jnp.concatenate([wuq_a, wuq_b], axis=1).astype(BF16)

    ukv = mla_w_ukv[l].reshape(KV_LORA, MLA_HEADS, MLA_NOPE + MLA_V)
    zk64 = jnp.zeros((KV_LORA, MLA_HEADS, HEAD_PAD - MLA_NOPE), F32)
    wuk = jnp.concatenate([ukv[..., :MLA_NOPE], zk64], axis=-1).reshape(KV_LORA, MLA_HEADS * HEAD_PAD)
    wuv = ukv[..., MLA_NOPE:].reshape(KV_LORA, MLA_HEADS * MLA_V)
    wukv = jnp.concatenate([wuk, wuv], axis=1).astype(BF16)

    cq, sq = _rope_tables(seq, mla_q_norm[l], MLA_QK ** -0.5 * LOG2_E)
    ck, sk = _rope_tables(seq, mla_k_norm[l], 1.0)

    lb = lower_bounds[l][None, :]
    r = jnp.arange(tm_hg)
    tri = ((r[:, None] // HG_CHUNK == r[None, :] // HG_CHUNK) & (r[:, None] >= r[None, :])).astype(BF16)
    nc = tm_hg // HG_CHUNK
    cmask = (jnp.arange(nc * HG_DV)[:, None] // HG_DV == r[None, :] // HG_CHUNK).astype(BF16)
    return dict(wmla=wmla, whg=whg, wgates=wgates, wuq=wuq, wukv=wukv, tabs=(cq, sq, ck, sk),
                loglb=jnp.log(lb), log1mlb=jnp.log1p(-lb), tri=tri, cmask=cmask)


def kernel(x, norm_mix, w_in, mla_norm_cq, mla_w_uq, mla_norm_ckv, mla_w_ukv, mla_q_norm, mla_k_norm,
           hg_lb_logits, hg_out_norm, w_proj_a, w_proj_b, w_out, norm_ffn, w_gate, w_up, w_down):
    batch, seq, d = x.shape
    assert d == D_MODEL
    depth = w_in.shape[0]
    tm = min(256, seq)
    tm_ffn = min(512, seq)
    assert seq % tm == 0 and seq % tm_ffn == 0 and tm % HG_CHUNK == 0

    p = jax.nn.softmax(hg_lb_logits.astype(F32), axis=0)
    lower_bounds = jnp.maximum(jnp.cumsum(p, axis=0) - p[0:1], 0.0)

    x2 = x.reshape(batch * seq, d)
    for l in range(depth):
        lp = _layer_params(l, seq, tm, w_in, mla_w_uq, mla_w_ukv, mla_q_norm, mla_k_norm, lower_bounds)
        gmix = norm_mix[l][None, :]
        q, k, vt = _mla_prep(x2, gmix, lp["wmla"], mla_norm_cq[l][None, :], lp["wuq"],
                             mla_norm_ckv[l][None, :], lp["wukv"], lp["tabs"], seq, tm)
        y_a = _mla_attn(q, k, vt, batch, seq, tm)
        y_b = _hgrn(x2, gmix, lp["whg"], lp["loglb"], lp["log1mlb"], hg_out_norm[l][None, :],
                    lp["tri"], lp["cmask"], batch, seq, tm)
        x2 = _merge(x2, y_a, y_b, gmix, lp["wgates"], w_proj_a[l].astype(BF16),
                    w_proj_b[l].astype(BF16), w_out[l].astype(BF16), tm_ffn)
        x2 = _ffn(x2, norm_ffn[l][None, :], w_gate[l].astype(BF16), w_up[l].astype(BF16),
                  w_down[l].astype(BF16), tm_ffn, 2)
    return x2.reshape(batch, seq, d)
```

```python
import functools

import jax
import jax.numpy as jnp
from jax import lax
from jax.experimental import pallas as pl
from jax.experimental.pallas import tpu as pltpu

D_MODEL = 1024
MLA_HEADS = 8
MLA_NOPE = 64
MLA_ROPE = 32
MLA_QK = MLA_NOPE + MLA_ROPE
MLA_V = 64
Q_LORA = 384
KV_LORA = 256
ROPE_THETA = 10000.0
HG_HEADS = 4
HG_DK = 128
HG_DV = 128
HG_KW = HG_HEADS * HG_DK
HG_WIDTH = HG_HEADS * HG_DV
NORM_EPS = 1e-6
IN_SPLITS = (Q_LORA, KV_LORA, MLA_ROPE, HG_KW, HG_KW, HG_WIDTH, HG_WIDTH, D_MODEL, D_MODEL)

LANES = 128
HEAD_PAD = LANES
ATTN_GROUP = 4
HG_CHUNK = 32
HG_SAFE_DECAY = 60.0
LOG2_E = 1.4426950408889634
VMEM_LIMIT = 56 * 1024 * 1024

BF16 = jnp.bfloat16
F32 = jnp.float32
NT = (((1,), (1,)), ((), ()))
TN = (((0,), (0,)), ((), ()))


def _rms(x, g):
    return x * lax.rsqrt(jnp.mean(x * x, axis=-1, keepdims=True) + NORM_EPS) * g


def _dot(a, b):
    return jnp.dot(a, b, preferred_element_type=F32)


def _const_spec(shape):
    nd = len(shape)
    return pl.BlockSpec(shape, lambda *_: (0,) * nd, pipeline_mode=pl.Buffered(1))


def _params(sem):
    return pltpu.CompilerParams(dimension_semantics=sem, vmem_limit_bytes=VMEM_LIMIT)


def _mla_prep_kernel(x_ref, gmix_ref, wmla_ref, gcq_ref, wuq_ref, gckv_ref, wukv_ref,
                     cq_ref, sq_ref, ck_ref, sk_ref, qt_ref, k_ref, vt_ref):
    h = _rms(x_ref[...], gmix_ref[...]).astype(BF16)
    zc = _dot(h, wmla_ref[...])
    c_q = zc[:, :Q_LORA]
    c_kv = zc[:, Q_LORA:Q_LORA + KV_LORA]
    kpe_a = zc[:, 640:768]
    kpe_b = zc[:, 768:896]
    q2 = _dot(_rms(c_q, gcq_ref[...]).astype(BF16), wuq_ref[...])
    kv2 = _dot(_rms(c_kv, gckv_ref[...]).astype(BF16), wukv_ref[...])
    cq, sq, ck, sk = cq_ref[...], sq_ref[...], ck_ref[...], sk_ref[...]
    inv_qk = 1.0 / MLA_QK
    for hd in range(MLA_HEADS):
        lo = hd * HEAD_PAD
        qa = q2[:, lo:lo + HEAD_PAD]
        qb = q2[:, MLA_HEADS * HEAD_PAD + lo:MLA_HEADS * HEAD_PAD + lo + HEAD_PAD]
        rq = lax.rsqrt(jnp.sum(qa * qa, axis=-1, keepdims=True) * inv_qk + NORM_EPS)
        qt_ref[lo:lo + HEAD_PAD, :] = ((qa * cq + qb * sq) * rq).T.astype(BF16)
        ka = kv2[:, lo:lo + HEAD_PAD] + kpe_a
        rk = lax.rsqrt(jnp.sum(ka * ka, axis=-1, keepdims=True) * inv_qk + NORM_EPS)
        k_ref[:, lo:lo + HEAD_PAD] = ((ka * ck + kpe_b * sk) * rk).astype(BF16)
    v = kv2[:, MLA_HEADS * HEAD_PAD:]
    gw = ATTN_GROUP * MLA_V
    for g in range(MLA_HEADS // ATTN_GROUP):
        vt_ref[g] = v[:, g * gw:(g + 1) * gw].T.astype(BF16)


def _mla_prep(x2, gmix, wmla, gcq, wuq, gckv, wukv, tabs, seq, tm):
    n = x2.shape[0]
    nt = seq // tm
    batch = n // seq
    tab_spec = pl.BlockSpec((tm, LANES), lambda i: (i % nt, 0))
    return pl.pallas_call(
        _mla_prep_kernel,
        grid=(n // tm,),
        in_specs=[
            pl.BlockSpec((tm, D_MODEL), lambda i: (i, 0)),
            _const_spec(gmix.shape), _const_spec(wmla.shape), _const_spec(gcq.shape),
            _const_spec(wuq.shape), _const_spec(gckv.shape), _const_spec(wukv.shape),
            tab_spec, tab_spec, tab_spec, tab_spec,
        ],
        out_specs=[
            pl.BlockSpec((None, None, MLA_HEADS * HEAD_PAD, tm), lambda i: (i // nt, i % nt, 0, 0)),
            pl.BlockSpec((tm, MLA_HEADS * HEAD_PAD), lambda i: (i, 0)),
            pl.BlockSpec((None, MLA_HEADS // ATTN_GROUP, None, ATTN_GROUP * MLA_V, tm),
                         lambda i: (i // nt, 0, i % nt, 0, 0)),
        ],
        out_shape=[
            jax.ShapeDtypeStruct((batch, nt, MLA_HEADS * HEAD_PAD, tm), BF16),
            jax.ShapeDtypeStruct((n, MLA_HEADS * HEAD_PAD), BF16),
            jax.ShapeDtypeStruct((batch, MLA_HEADS // ATTN_GROUP, nt, ATTN_GROUP * MLA_V, tm), BF16),
        ],
        compiler_params=_params(("parallel",)),
        name="mla_prep",
    )(x2, gmix, wmla, gcq, wuq, gckv, wukv, *tabs)


def _attn_kernel(qt_ref, k_ref, vt_ref, o_ref, s0_scr, m0_scr, *, tq):
    i = pl.program_id(2)
    nq = pl.num_programs(2)
    kv_idx = lax.broadcasted_iota(jnp.int32, (tq, tq), 0)
    q_idx = lax.broadcasted_iota(jnp.int32, (tq, tq), 1)
    causal = q_idx >= kv_idx

    def scores(j, iq):
        start = pl.multiple_of(j * tq, tq)
        out = []
        for hh in range(ATTN_GROUP):
            sc = _dot(k_ref[pl.ds(start, tq), hh * HEAD_PAD:(hh + 1) * HEAD_PAD],
                      qt_ref[iq, hh * HEAD_PAD:(hh + 1) * HEAD_PAD, :])
            out.append((sc, jnp.max(sc, axis=0, keepdims=True)))
        return tuple(out)

    def update(j, s_all, carry, masked):
        vt = vt_ref[j]
        out = []
        for hh in range(ATTN_GROUP):
            m, l, acc = carry[hh]
            s, s_max = s_all[hh]
            if masked:
                s = jnp.where(causal, s, -jnp.inf)
                s_max = jnp.max(s, axis=0, keepdims=True)
            m_new = jnp.maximum(m, s_max)
            alpha = jnp.exp2(m - m_new)
            p = jnp.exp2(s - m_new)
            l = alpha * l + jnp.sum(p, axis=0, keepdims=True)
            acc = alpha * acc + _dot(vt[hh * MLA_V:(hh + 1) * MLA_V, :], p.astype(BF16))
            out.append((m_new, l, acc))
        return tuple(out)

    def stash(s_all):
        for hh in range(ATTN_GROUP):
            s0_scr[hh] = s_all[hh][0]
            m0_scr[hh] = s_all[hh][1]

    @pl.when(i == 0)
    def _():
        stash(scores(0, 0))

    first = tuple((s0_scr[hh], m0_scr[hh]) for hh in range(ATTN_GROUP))
    init = tuple((jnp.full((1, tq), -jnp.inf, F32), jnp.zeros((1, tq), F32), jnp.zeros((MLA_V, tq), F32))
                 for _ in range(ATTN_GROUP))

    def body(j, c):
        s_next = scores(j + 1, i)
        return s_next, update(j, c[0], c[1], False)

    s_last, carry = lax.fori_loop(0, i, body, (first, init))
    s_ahead = scores(0, jnp.minimum(i + 1, nq - 1))
    carry = update(i, s_last, carry, True)
    stash(s_ahead)
    o_t = jnp.concatenate([acc / l for _, l, acc in carry], axis=0)
    o_ref[...] = o_t.T.astype(BF16)


def _mla_attn(qt, k, vt, batch, seq, tq):
    nq = seq // tq
    return pl.pallas_call(
        functools.partial(_attn_kernel, tq=tq),
        grid=(batch, MLA_HEADS // ATTN_GROUP, nq),
        in_specs=[
            pl.BlockSpec((None, nq, ATTN_GROUP * HEAD_PAD, tq), lambda b, g, i: (b, 0, g, 0)),
            pl.BlockSpec((seq, ATTN_GROUP * HEAD_PAD), lambda b, g, i: (b, g)),
            pl.BlockSpec((None, None, nq, ATTN_GROUP * MLA_V, tq), lambda b, g, i: (b, g, 0, 0, 0)),
        ],
        out_specs=pl.BlockSpec((tq, ATTN_GROUP * MLA_V), lambda b, g, i: (b * nq + i, g)),
        out_shape=jax.ShapeDtypeStruct((batch * seq, MLA_HEADS * MLA_V), BF16),
        scratch_shapes=[
            pltpu.VMEM((ATTN_GROUP, tq, tq), F32),
            pltpu.VMEM((ATTN_GROUP, 1, tq), F32),
        ],
        compiler_params=_params(("parallel", "parallel", "arbitrary")),
        name="mla_attn",
    )(qt, k, vt)


def _split3(x):
    a = x.astype(BF16)
    r = x - a.astype(F32)
    b = r.astype(BF16)
    c = (r - b.astype(F32)).astype(BF16)
    return a, b, c


def _hgrn_kernel(x_ref, gmix_ref, whg_ref, loglb_ref, log1mlb_ref, gout_ref, tri_ref, cmask_ref, o_ref,
                 st_ref, stbak_ref, q_scr, k_scr, v_scr, b_scr, o_scr, *, tm):
    nc = tm // HG_CHUNK
    C = HG_CHUNK

    @pl.when(pl.program_id(1) == 0)
    def _():
        st_ref[...] = jnp.zeros_like(st_ref)

    h = _rms(x_ref[...], gmix_ref[...]).astype(BF16)
    z = _dot(h, whg_ref[...])
    zq = z[:, :HG_KW]
    zf = z[:, HG_KW:2 * HG_KW]
    zi = z[:, 2 * HG_KW:2 * HG_KW + HG_WIDTH]
    zg = z[:, 2 * HG_KW + HG_WIDTH:]
    log_sig = jnp.minimum(zf, 0.0) - jnp.log(1.0 + jnp.exp(-jnp.abs(zf)))
    a = loglb_ref[...]
    c = log1mlb_ref[...] + log_sig
    log_f = jnp.maximum(a, c) + jnp.log(1.0 + jnp.exp(-jnp.abs(a - c)))
    kk = 1.0 - jnp.exp(log_f)
    tri = tri_ref[...]
    l1, l2, l3 = _split3(log_f)
    b = _dot(tri, l1) + _dot(tri, l2) + _dot(tri, l3)
    b3 = b.reshape(nc, C, HG_KW)
    b_last = b3[:, C - 1:C, :]
    q_scr[...] = zq
    k_scr[...] = kk
    v_scr[...] = zi
    b_scr[...] = b
    stbak_ref[...] = st_ref[...]
    unsafe = jnp.min(b) < -HG_SAFE_DECAY

    qe = (zq * jnp.exp(b)).astype(BF16)
    ke = (kk * jnp.exp(jnp.minimum(-b, HG_SAFE_DECAY + 20.0))).astype(BF16)
    kl = (kk.reshape(nc, C, HG_KW) * jnp.exp(b_last - b3)).reshape(tm, HG_KW).astype(BF16)
    e_last = jnp.exp(b_last)
    vb = zi.astype(BF16)
    vt = zi.T.astype(BF16)
    row = lax.broadcasted_iota(jnp.int32, (tm, tm), 0)
    col = lax.broadcasted_iota(jnp.int32, (tm, tm), 1)
    causal = (row >= col) & (row // C == col // C)
    cmask = cmask_ref[...]
    for hd in range(HG_HEADS):
        lo = hd * HG_DK
        qe_h = qe[:, lo:lo + HG_DK]
        att = lax.dot_general(qe_h, ke[:, lo:lo + HG_DK], NT, preferred_element_type=F32)
        att = jnp.where(causal, att, 0.0).astype(BF16)
        o_intra = _dot(att, vb[:, lo:lo + HG_DV])
        vt_h = vt[lo:lo + HG_DV, :]
        upd = _dot(jnp.concatenate([vt_h] * nc, axis=0) * cmask, kl[:, lo:lo + HG_DK])
        st = st_ref[hd]
        for ci in range(nc):
            r0 = ci * C
            o_inter = lax.dot_general(qe_h[r0:r0 + C], st.astype(BF16), NT, preferred_element_type=F32)
            o_scr[r0:r0 + C, lo:lo + HG_DV] = o_intra[r0:r0 + C] + o_inter
            st = st * e_last[ci, :, lo:lo + HG_DK] + upd[ci * HG_DV:(ci + 1) * HG_DV]
        st_ref[hd] = st

    @pl.when(unsafe)
    def _():
        st_ref[...] = stbak_ref[...]
        t_idx = lax.broadcasted_iota(jnp.int32, (C, 1), 0)
        for hd in range(HG_HEADS):
            lo = hd * HG_DK

            def chunk_body(ci, _, lo=lo, hd=hd):
                r0 = pl.multiple_of(ci * C, C)
                q_c = q_scr[pl.ds(r0, C), lo:lo + HG_DK]
                k_c = k_scr[pl.ds(r0, C), lo:lo + HG_DK]
                v_c = v_scr[pl.ds(r0, C), lo:lo + HG_DV]
                b_c = b_scr[pl.ds(r0, C), lo:lo + HG_DK]
                bl = b_c[C - 1:C, :]

                def s_body(s, o_acc):
                    pick = t_idx == s
                    ks = jnp.sum(jnp.where(pick, k_c, 0.0), axis=0, keepdims=True)
                    bs = jnp.sum(jnp.where(pick, b_c, 0.0), axis=0, keepdims=True)
                    vs = jnp.sum(jnp.where(pick, v_c, 0.0), axis=0, keepdims=True)
                    dec = jnp.exp(jnp.minimum(b_c - bs, 0.0))
                    w = jnp.sum(q_c * ks * dec, axis=-1, keepdims=True)
                    w = jnp.where(t_idx >= s, w, 0.0)
                    return o_acc + w * vs

                o_intra = lax.fori_loop(0, C, s_body, jnp.zeros((C, HG_DV), F32))
                st = st_ref[hd]
                qe_c = (q_c * jnp.exp(b_c)).astype(BF16)
                o_inter = lax.dot_general(qe_c, st.astype(BF16), NT, preferred_element_type=F32)
                o_scr[pl.ds(r0, C), lo:lo + HG_DV] = o_intra + o_inter
                kl_c = (k_c * jnp.exp(bl - b_c)).astype(BF16)
                upd = lax.dot_general(v_c.astype(BF16), kl_c, TN, preferred_element_type=F32)
                st_ref[hd] = st * jnp.exp(bl) + upd
                return 0

            lax.fori_loop(0, nc, chunk_body, 0)

    o = o_scr[...]
    gate = zg * jax.nn.sigmoid(zg)
    gout = gout_ref[...]
    for hd in range(HG_HEADS):
        lo = hd * HG_DV
        o_ref[:, lo:lo + HG_DV] = (_rms(o[:, lo:lo + HG_DV], gout) * gate[:, lo:lo + HG_DV]).astype(BF16)


def _hgrn(x2, gmix, whg, loglb, log1mlb, gout, tri, cmask, batch, seq, tm):
    nt = seq // tm
    return pl.pallas_call(
        functools.partial(_hgrn_kernel, tm=tm),
        grid=(batch, nt),
        in_specs=[
            pl.BlockSpec((tm, D_MODEL), lambda b, i: (b * nt + i, 0)),
            _const_spec(gmix.shape), _const_spec(whg.shape), _const_spec(loglb.shape),
            _const_spec(log1mlb.shape), _const_spec(gout.shape), _const_spec(tri.shape),
            _const_spec(cmask.shape),
        ],
        out_specs=pl.BlockSpec((tm, HG_WIDTH), lambda b, i: (b * nt + i, 0)),
        out_shape=jax.ShapeDtypeStruct((batch * seq, HG_WIDTH), BF16),
        scratch_shapes=[
            pltpu.VMEM((HG_HEADS, HG_DV, HG_DK), F32),
            pltpu.VMEM((HG_HEADS, HG_DV, HG_DK), F32),
            pltpu.VMEM((tm, HG_KW), F32),
            pltpu.VMEM((tm, HG_KW), F32),
            pltpu.VMEM((tm, HG_WIDTH), F32),
            pltpu.VMEM((tm, HG_KW), F32),
            pltpu.VMEM((tm, HG_WIDTH), F32),
        ],
        compiler_params=_params(("parallel", "arbitrary")),
        name="hgrn",
    )(x2, gmix, whg, loglb, log1mlb, gout, tri, cmask)


def _merge_kernel(x_ref, ya_ref, yb_ref, gmix_ref, wg_ref, wpa_ref, wpb_ref, wout_ref, o_ref):
    x = x_ref[...]
    h = _rms(x, gmix_ref[...]).astype(BF16)
    g = _dot(h, wg_ref[...])
    y_a = _dot(ya_ref[...], wpa_ref[...])
    y_b = _dot(yb_ref[...], wpb_ref[...])
    y = jax.nn.sigmoid(g[:, :D_MODEL]) * y_a + jax.nn.sigmoid(g[:, D_MODEL:]) * y_b
    o_ref[...] = x + _dot(y.astype(BF16), wout_ref[...])


def _merge(x2, ya, yb, gmix, wg, wpa, wpb, wout, tm):
    n = x2.shape[0]
    return pl.pallas_call(
        _merge_kernel,
        grid=(n // tm,),
        in_specs=[
            pl.BlockSpec((tm, D_MODEL), lambda i: (i, 0)),
            pl.BlockSpec((tm, ya.shape[1]), lambda i: (i, 0)),
            pl.BlockSpec((tm, yb.shape[1]), lambda i: (i, 0)),
            _const_spec(gmix.shape), _const_spec(wg.shape), _const_spec(wpa.shape),
            _const_spec(wpb.shape), _const_spec(wout.shape),
        ],
        out_specs=pl.BlockSpec((tm, D_MODEL), lambda i: (i, 0)),
        out_shape=jax.ShapeDtypeStruct((n, D_MODEL), F32),
        compiler_params=_params(("parallel",)),
        name="merge",
    )(x2, ya, yb, gmix, wg, wpa, wpb, wout)


def _ffn_kernel(x_ref, gffn_ref, wgate_ref, wup_ref, wdown_ref, o_ref, *, n_split):
    x = x_ref[...]
    h = _rms(x, gffn_ref[...]).astype(BF16)
    d_ff = wgate_ref.shape[1]
    w = d_ff // n_split
    acc = x
    for j in range(n_split):
        gate = _dot(h, wgate_ref[:, j * w:(j + 1) * w])
        up = _dot(h, wup_ref[:, j * w:(j + 1) * w])
        act = (gate * jax.nn.sigmoid(gate) * up).astype(BF16)
        acc = acc + _dot(act, wdown_ref[j * w:(j + 1) * w, :])
    o_ref[...] = acc


def _ffn(x2, gffn, wgate, wup, wdown, tm, n_split):
    n = x2.shape[0]
    return pl.pallas_call(
        functools.partial(_ffn_kernel, n_split=n_split),
        grid=(n // tm,),
        in_specs=[
            pl.BlockSpec((tm, D_MODEL), lambda i: (i, 0)),
            _const_spec(gffn.shape), _const_spec(wgate.shape), _const_spec(wup.shape),
            _const_spec(wdown.shape),
        ],
        out_specs=pl.BlockSpec((tm, D_MODEL), lambda i: (i, 0)),
        out_shape=jax.ShapeDtypeStruct((n, D_MODEL), F32),
        compiler_params=_params(("parallel",)),
        name="ffn",
    )(x2, gffn, wgate, wup, wdown)


def _rope_tables(seq, gain, scale):
    pos = jnp.arange(seq, dtype=F32)
    inv_freq = 1.0 / (ROPE_THETA ** (jnp.arange(0, MLA_ROPE, 2, dtype=F32) / MLA_ROPE))
    ang = pos[:, None] * inv_freq[None, :]
    cos, sin = jnp.cos(ang), jnp.sin(ang)
    half = MLA_ROPE // 2
    g_nope, g_rope = gain[:MLA_NOPE], gain[MLA_NOPE:]
    pad = jnp.zeros((seq, HEAD_PAD - MLA_QK), F32)
    ctab = jnp.concatenate([
        jnp.broadcast_to(g_nope[None, :], (seq, MLA_NOPE)),
        jnp.concatenate([cos, cos], axis=1) * g_rope[None, :], pad], axis=1)
    g_partner = jnp.concatenate([g_rope[half:], g_rope[:half]])
    stab = jnp.concatenate([
        jnp.zeros((seq, MLA_NOPE), F32),
        jnp.concatenate([-sin, sin], axis=1) * g_partner[None, :], pad], axis=1)
    return ctab * scale, stab * scale


def _layer_params(l, seq, tm_hg, w_in, mla_w_uq, mla_w_ukv, mla_q_norm, mla_k_norm, lower_bounds):
    o = [0]
    for s in IN_SPLITS:
        o.append(o[-1] + s)
    wi = w_in[l]
    half = MLA_ROPE // 2
    z64 = jnp.zeros((D_MODEL, MLA_NOPE), F32)
    z32 = jnp.zeros((D_MODEL, HEAD_PAD - MLA_QK), F32)
    w_kpe = wi[:, o[2]:o[3]]
    w_kpe_sw = jnp.concatenate([w_kpe[:, half:], w_kpe[:, :half]], axis=1)
    wmla = jnp.concatenate([wi[:, o[0]:o[2]], z64, w_kpe, z32, z64, w_kpe_sw, z32], axis=1).astype(BF16)
    whg = wi[:, o[3]:o[7]].astype(BF16)
    wgates = wi[:, o[7]:o[9]].astype(BF16)

    uq = mla_w_uq[l].reshape(Q_LORA, MLA_HEADS, MLA_QK)
    uq_nope, uq_rope = uq[..., :MLA_NOPE], uq[..., MLA_NOPE:]
    uq_rope_sw = jnp.concatenate([uq_rope[..., half:], uq_rope[..., :half]], axis=-1)
    zq32 = jnp.zeros((Q_LORA, MLA_HEADS, HEAD_PAD - MLA_QK), F32)
    zq64 = jnp.zeros((Q_LORA, MLA_HEADS, MLA_NOPE), F32)
    wuq_a = jnp.concatenate([uq_nope, uq_rope, zq32], axis=-1).reshape(Q_LORA, MLA_HEADS * HEAD_PAD)
    wuq_b = jnp.concatenate([zq64, uq_rope_sw, zq32], axis=-1).reshape(Q_LORA, MLA_HEADS * HEAD_PAD)
    wuq = jnp.concatenate([wuq_a, wuq_b], axis=1).astype(BF16)

    ukv = mla_w_ukv[l].reshape(KV_LORA, MLA_HEADS, MLA_NOPE + MLA_V)
    zk64 = jnp.zeros((KV_LORA, MLA_HEADS, HEAD_PAD - MLA_NOPE), F32)
    wuk = jnp.concatenate([ukv[..., :MLA_NOPE], zk64], axis=-1).reshape(KV_LORA, MLA_HEADS * HEAD_PAD)
    wuv = ukv[..., MLA_NOPE:].reshape(KV_LORA, MLA_HEADS * MLA_V)
    wukv = jnp.concatenate([wuk, wuv], axis=1).astype(BF16)

    cq, sq = _rope_tables(seq, mla_q_norm[l], MLA_QK ** -0.5 * LOG2_E)
    ck, sk = _rope_tables(seq, mla_k_norm[l], 1.0)

    lb = lower_bounds[l][None, :]
    r = jnp.arange(tm_hg)
    tri = ((r[:, None] // HG_CHUNK == r[None, :] // HG_CHUNK) & (r[:, None] >= r[None, :])).astype(BF16)
    nc = tm_hg // HG_CHUNK
    cmask = (jnp.arange(nc * HG_DV)[:, None] // HG_DV == r[None, :] // HG_CHUNK).astype(BF16)
    return dict(wmla=wmla, whg=whg, wgates=wgates, wuq=wuq, wukv=wukv, tabs=(cq, sq, ck, sk),
                loglb=jnp.log(lb), log1mlb=jnp.log1p(-lb), tri=tri, cmask=cmask)


def kernel(x, norm_mix, w_in, mla_norm_cq, mla_w_uq, mla_norm_ckv, mla_w_ukv, mla_q_norm, mla_k_norm,
           hg_lb_logits, hg_out_norm, w_proj_a, w_proj_b, w_out, norm_ffn, w_gate, w_up, w_down):
    batch, seq, d = x.shape
    assert d == D_MODEL
    depth = w_in.shape[0]
    tm = min(256, seq)
    tm_ffn = min(512, seq)
    assert seq % tm == 0 and seq % tm_ffn == 0 and tm % HG_CHUNK == 0

    p = jax.nn.softmax(hg_lb_logits.astype(F32), axis=0)
    lower_bounds = jnp.maximum(jnp.cumsum(p, axis=0) - p[0:1], 0.0)

    x2 = x.reshape(batch * seq, d)
    for l in range(depth):
        lp = _layer_params(l, seq, tm, w_in, mla_w_uq, mla_w_ukv, mla_q_norm, mla_k_norm, lower_bounds)
        gmix = norm_mix[l][None, :]
        qt, k, vt = _mla_prep(x2, gmix, lp["wmla"], mla_norm_cq[l][None, :], lp["wuq"],
                             mla_norm_ckv[l][None, :], lp["wukv"], lp["tabs"], seq, tm)
        y_a = _mla_attn(qt, k, vt, batch, seq, tm)
        y_b = _hgrn(x2, gmix, lp["whg"], lp["loglb"], lp["log1mlb"], hg_out_norm[l][None, :],
                    lp["tri"], lp["cmask"], batch, seq, tm)
        x2 = _merge(x2, y_a, y_b, gmix, lp["wgates"], w_proj_a[l].astype(BF16),
                    w_proj_b[l].astype(BF16), w_out[l].astype(BF16), tm_ffn)
        x2 = _ffn(x2, norm_ffn[l][None, :], w_gate[l].astype(BF16), w_up[l].astype(BF16),
                  w_down[l].astype(BF16), tm_ffn, 2)
    return x2.reshape(batch, seq, d)
```

```python
import functools

import jax
import jax.numpy as jnp
from jax import lax
from jax.experimental import pallas as pl
from jax.experimental.pallas import tpu as pltpu

D_MODEL = 1024
MLA_HEADS = 8
MLA_NOPE = 64
MLA_ROPE = 32
MLA_QK = MLA_NOPE + MLA_ROPE
MLA_V = 64
Q_LORA = 384
KV_LORA = 256
ROPE_THETA = 10000.0
HG_HEADS = 4
HG_DK = 128
HG_DV = 128
HG_KW = HG_HEADS * HG_DK
HG_WIDTH = HG_HEADS * HG_DV
NORM_EPS = 1e-6
IN_SPLITS = (Q_LORA, KV_LORA, MLA_ROPE, HG_KW, HG_KW, HG_WIDTH, HG_WIDTH, D_MODEL, D_MODEL)

LANES = 128
HEAD_PAD = LANES
ATTN_GROUP = 4
HG_CHUNK = 32
HG_SAFE_DECAY = 60.0
LOG2_E = 1.4426950408889634
MERGE_SPLIT = 2
FFN_SPLIT = 11
VMEM_LIMIT = 56 * 1024 * 1024

BF16 = jnp.bfloat16
F32 = jnp.float32
NT = (((1,), (1,)), ((), ()))
TN = (((0,), (0,)), ((), ()))


def _rms(x, g):
    return x * lax.rsqrt(jnp.mean(x * x, axis=-1, keepdims=True) + NORM_EPS) * g


def _dot(a, b):
    return jnp.dot(a, b, preferred_element_type=F32)


def _const_spec(shape):
    nd = len(shape)
    return pl.BlockSpec(shape, lambda *_: (0,) * nd, pipeline_mode=pl.Buffered(1))


def _params(sem):
    return pltpu.CompilerParams(dimension_semantics=sem, vmem_limit_bytes=VMEM_LIMIT)


def _mla_prep_kernel(x_ref, gmix_ref, wmla_ref, gcq_ref, wuq_ref, gckv_ref, wukv_ref, nmask_ref,
                     tq_ref, ck_ref, sk_ref, qt_ref, k_ref, vt_ref):
    h = _rms(x_ref[...], gmix_ref[...]).astype(BF16)
    zc = _dot(h, wmla_ref[...])
    c_q = zc[:, :Q_LORA]
    c_kv = zc[:, Q_LORA:Q_LORA + KV_LORA]
    kpe_a = zc[:, 640:768]
    kpe_b = zc[:, 768:896]
    q2 = _dot(_rms(c_q, gcq_ref[...]).astype(BF16), wuq_ref[...])
    kv2 = _dot(_rms(c_kv, gckv_ref[...]).astype(BF16), wukv_ref[...])
    tq_tab, ck, sk, nmask = tq_ref[...], ck_ref[...], sk_ref[...], nmask_ref[...]
    inv_qk = 1.0 / MLA_QK
    for hd in range(MLA_HEADS):
        lo = hd * HEAD_PAD
        qa = q2[:, lo:lo + HEAD_PAD]
        rq = lax.rsqrt(jnp.sum(qa * qa * nmask, axis=-1, keepdims=True) * inv_qk + NORM_EPS)
        qt_ref[lo:lo + HEAD_PAD, :] = (qa * tq_tab * rq).astype(BF16).T
        ka = kv2[:, lo:lo + HEAD_PAD] + kpe_a
        rk = lax.rsqrt(jnp.sum(ka * ka * nmask, axis=-1, keepdims=True) * inv_qk + NORM_EPS)
        k_ref[:, lo:lo + HEAD_PAD] = ((ka * ck + kpe_b * sk) * rk).astype(BF16)
    v = kv2[:, MLA_HEADS * HEAD_PAD:]
    gw = ATTN_GROUP * MLA_V
    for g in range(MLA_HEADS // ATTN_GROUP):
        vt_ref[g] = v[:, g * gw:(g + 1) * gw].astype(BF16).T


def _mla_prep(x2, gmix, wmla, gcq, wuq, gckv, wukv, nmask, tabs, seq, tm):
    n = x2.shape[0]
    nt = seq // tm
    batch = n // seq
    tab_spec = pl.BlockSpec((tm, LANES), lambda i: (i % nt, 0))
    return pl.pallas_call(
        _mla_prep_kernel,
        grid=(n // tm,),
        in_specs=[
            pl.BlockSpec((tm, D_MODEL), lambda i: (i, 0)),
            _const_spec(gmix.shape), _const_spec(wmla.shape), _const_spec(gcq.shape),
            _const_spec(wuq.shape), _const_spec(gckv.shape), _const_spec(wukv.shape),
            _const_spec(nmask.shape), tab_spec, tab_spec, tab_spec,
        ],
        out_specs=[
            pl.BlockSpec((None, None, MLA_HEADS * HEAD_PAD, tm), lambda i: (i // nt, i % nt, 0, 0)),
            pl.BlockSpec((tm, MLA_HEADS * HEAD_PAD), lambda i: (i, 0)),
            pl.BlockSpec((None, MLA_HEADS // ATTN_GROUP, None, ATTN_GROUP * MLA_V, tm),
                         lambda i: (i // nt, 0, i % nt, 0, 0)),
        ],
        out_shape=[
            jax.ShapeDtypeStruct((batch, nt, MLA_HEADS * HEAD_PAD, tm), BF16),
            jax.ShapeDtypeStruct((n, MLA_HEADS * HEAD_PAD), BF16),
            jax.ShapeDtypeStruct((batch, MLA_HEADS // ATTN_GROUP, nt, ATTN_GROUP * MLA_V, tm), BF16),
        ],
        compiler_params=_params(("parallel",)),
        name="mla_prep",
    )(x2, gmix, wmla, gcq, wuq, gckv, wukv, nmask, *tabs)


def _attn_kernel(qt_ref, k_ref, vt_ref, o_ref, s0_scr, m0_scr, *, tq):
    i = pl.program_id(2)
    nq = pl.num_programs(2)
    kv_idx = lax.broadcasted_iota(jnp.int32, (tq, tq), 0)
    q_idx = lax.broadcasted_iota(jnp.int32, (tq, tq), 1)
    causal = q_idx >= kv_idx

    def scores(j, iq):
        start = pl.multiple_of(j * tq, tq)
        out = []
        for hh in range(ATTN_GROUP):
            sc = _dot(k_ref[pl.ds(start, tq), hh * HEAD_PAD:(hh + 1) * HEAD_PAD],
                      qt_ref[iq, hh * HEAD_PAD:(hh + 1) * HEAD_PAD, :])
            out.append((sc, jnp.max(sc, axis=0, keepdims=True)))
        return tuple(out)

    def update(j, s_all, carry, masked):
        vt = vt_ref[j]
        out = []
        for hh in range(ATTN_GROUP):
            m, l, acc = carry[hh]
            s, s_max = s_all[hh]
            if masked:
                s = jnp.where(causal, s, -jnp.inf)
                s_max = jnp.max(s, axis=0, keepdims=True)
            m_new = jnp.maximum(m, s_max)
            alpha = jnp.exp2(m - m_new)
            p = jnp.exp2(s - m_new)
            l = alpha * l + jnp.sum(p, axis=0, keepdims=True)
            acc = alpha * acc + _dot(vt[hh * MLA_V:(hh + 1) * MLA_V, :], p.astype(BF16))
            out.append((m_new, l, acc))
        return tuple(out)

    def stash(s_all):
        for hh in range(ATTN_GROUP):
            s0_scr[hh] = s_all[hh][0]
            m0_scr[hh] = s_all[hh][1]

    @pl.when(i == 0)
    def _():
        stash(scores(0, 0))

    first = tuple((s0_scr[hh], m0_scr[hh]) for hh in range(ATTN_GROUP))
    init = tuple((jnp.full((1, tq), -jnp.inf, F32), jnp.zeros((1, tq), F32), jnp.zeros((MLA_V, tq), F32))
                 for _ in range(ATTN_GROUP))

    def body(j, c):
        s_next = scores(j + 1, i)
        return s_next, update(j, c[0], c[1], False)

    s_last, carry = lax.fori_loop(0, i, body, (first, init))
    s_ahead = scores(0, jnp.minimum(i + 1, nq - 1))
    carry = update(i, s_last, carry, True)
    stash(s_ahead)
    o_t = jnp.concatenate([acc / l for _, l, acc in carry], axis=0)
    o_ref[...] = o_t.T.astype(BF16)


def _mla_attn(qt, k, vt, batch, seq, tq):
    nq = seq // tq
    return pl.pallas_call(
        functools.partial(_attn_kernel, tq=tq),
        grid=(batch, MLA_HEADS // ATTN_GROUP, nq),
        in_specs=[
            pl.BlockSpec((None, nq, ATTN_GROUP * HEAD_PAD, tq), lambda b, g, i: (b, 0, g, 0)),
            pl.BlockSpec((seq, ATTN_GROUP * HEAD_PAD), lambda b, g, i: (b, g)),
            pl.BlockSpec((None, None, nq, ATTN_GROUP * MLA_V, tq), lambda b, g, i: (b, g, 0, 0, 0)),
        ],
        out_specs=pl.BlockSpec((tq, ATTN_GROUP * MLA_V), lambda b, g, i: (b * nq + i, g)),
        out_shape=jax.ShapeDtypeStruct((batch * seq, MLA_HEADS * MLA_V), BF16),
        scratch_shapes=[
            pltpu.VMEM((ATTN_GROUP, tq, tq), F32),
            pltpu.VMEM((ATTN_GROUP, 1, tq), F32),
        ],
        compiler_params=_params(("parallel", "parallel", "arbitrary")),
        name="mla_attn",
    )(qt, k, vt)


def _split3(x):
    a = x.astype(BF16)
    r = x - a.astype(F32)
    b = r.astype(BF16)
    c = (r - b.astype(F32)).astype(BF16)
    return a, b, c


def _hgrn_kernel(x_ref, gmix_ref, whg_ref, loglb_ref, log1mlb_ref, gout_ref, tri_ref, cmask_ref, o_ref,
                 st_ref, stbak_ref, q_scr, k_scr, v_scr, b_scr, o_scr, *, tm):
    nc = tm // HG_CHUNK
    C = HG_CHUNK

    @pl.when(pl.program_id(1) == 0)
    def _():
        st_ref[...] = jnp.zeros_like(st_ref)

    h = _rms(x_ref[...], gmix_ref[...]).astype(BF16)
    z = _dot(h, whg_ref[...])
    zq = z[:, :HG_KW]
    zf = z[:, HG_KW:2 * HG_KW]
    zi = z[:, 2 * HG_KW:2 * HG_KW + HG_WIDTH]
    zg = z[:, 2 * HG_KW + HG_WIDTH:]
    log_sig = jnp.minimum(zf, 0.0) - jnp.log(1.0 + jnp.exp(-jnp.abs(zf)))
    a = loglb_ref[...]
    c = log1mlb_ref[...] + log_sig
    log_f = jnp.maximum(a, c) + jnp.log(1.0 + jnp.exp(-jnp.abs(a - c)))
    kk = 1.0 - jnp.exp(log_f)
    tri = tri_ref[...]
    l1, l2, l3 = _split3(log_f)
    b = _dot(tri, l1) + _dot(tri, l2) + _dot(tri, l3)
    b3 = b.reshape(nc, C, HG_KW)
    b_last = b3[:, C - 1:C, :]
    q_scr[...] = zq
    k_scr[...] = kk
    v_scr[...] = zi
    b_scr[...] = b
    stbak_ref[...] = st_ref[...]
    unsafe = jnp.min(b) < -HG_SAFE_DECAY

    qe = (zq * jnp.exp(b)).astype(BF16)
    ke = (kk * jnp.exp(jnp.minimum(-b, HG_SAFE_DECAY + 20.0))).astype(BF16)
    kl = (kk.reshape(nc, C, HG_KW) * jnp.exp(b_last - b3)).reshape(tm, HG_KW).astype(BF16)
    e_last = jnp.exp(b_last)
    vb = zi.astype(BF16)
    vt = vb.T
    row = lax.broadcasted_iota(jnp.int32, (tm, tm), 0)
    col = lax.broadcasted_iota(jnp.int32, (tm, tm), 1)
    causal = (row >= col) & (row // C == col // C)
    cmask = cmask_ref[...]
    for hd in range(HG_HEADS):
        lo = hd * HG_DK
        qe_h = qe[:, lo:lo + HG_DK]
        att = lax.dot_general(qe_h, ke[:, lo:lo + HG_DK], NT, preferred_element_type=F32)
        att = jnp.where(causal, att, 0.0).astype(BF16)
        o_intra = _dot(att, vb[:, lo:lo + HG_DV])
        vt_h = vt[lo:lo + HG_DV, :]
        upd = _dot(jnp.concatenate([vt_h] * nc, axis=0) * cmask, kl[:, lo:lo + HG_DK])
        st = st_ref[hd]
        for ci in range(nc):
            r0 = ci * C
            o_inter = lax.dot_general(qe_h[r0:r0 + C], st.astype(BF16), NT, preferred_element_type=F32)
            o_scr[r0:r0 + C, lo:lo + HG_DV] = o_intra[r0:r0 + C] + o_inter
            st = st * e_last[ci, :, lo:lo + HG_DK] + upd[ci * HG_DV:(ci + 1) * HG_DV]
        st_ref[hd] = st

    @pl.when(unsafe)
    def _():
        st_ref[...] = stbak_ref[...]
        t_idx = lax.broadcasted_iota(jnp.int32, (C, 1), 0)
        for hd in range(HG_HEADS):
            lo = hd * HG_DK

            def chunk_body(ci, _, lo=lo, hd=hd):
                r0 = pl.multiple_of(ci * C, C)
                q_c = q_scr[pl.ds(r0, C), lo:lo + HG_DK]
                k_c = k_scr[pl.ds(r0, C), lo:lo + HG_DK]
                v_c = v_scr[pl.ds(r0, C), lo:lo + HG_DV]
                b_c = b_scr[pl.ds(r0, C), lo:lo + HG_DK]
                bl = b_c[C - 1:C, :]

                def s_body(s, o_acc):
                    pick = t_idx == s
                    ks = jnp.sum(jnp.where(pick, k_c, 0.0), axis=0, keepdims=True)
                    bs = jnp.sum(jnp.where(pick, b_c, 0.0), axis=0, keepdims=True)
                    vs = jnp.sum(jnp.where(pick, v_c, 0.0), axis=0, keepdims=True)
                    dec = jnp.exp(jnp.minimum(b_c - bs, 0.0))
                    w = jnp.sum(q_c * ks * dec, axis=-1, keepdims=True)
                    w = jnp.where(t_idx >= s, w, 0.0)
                    return o_acc + w * vs

                o_intra = lax.fori_loop(0, C, s_body, jnp.zeros((C, HG_DV), F32))
                st = st_ref[hd]
                qe_c = (q_c * jnp.exp(b_c)).astype(BF16)
                o_inter = lax.dot_general(qe_c, st.astype(BF16), NT, preferred_element_type=F32)
                o_scr[pl.ds(r0, C), lo:lo + HG_DV] = o_intra + o_inter
                kl_c = (k_c * jnp.exp(bl - b_c)).astype(BF16)
                upd = lax.dot_general(v_c.astype(BF16), kl_c, TN, preferred_element_type=F32)
                st_ref[hd] = st * jnp.exp(bl) + upd
                return 0

            lax.fori_loop(0, nc, chunk_body, 0)

    o = o_scr[...]
    gate = zg * jax.nn.sigmoid(zg)
    gout = gout_ref[...]
    for hd in range(HG_HEADS):
        lo = hd * HG_DV
        o_ref[:, lo:lo + HG_DV] = (_rms(o[:, lo:lo + HG_DV], gout) * gate[:, lo:lo + HG_DV]).astype(BF16)


def _hgrn(x2, gmix, whg, loglb, log1mlb, gout, tri, cmask, batch, seq, tm):
    nt = seq // tm
    return pl.pallas_call(
        functools.partial(_hgrn_kernel, tm=tm),
        grid=(batch, nt),
        in_specs=[
            pl.BlockSpec((tm, D_MODEL), lambda b, i: (b * nt + i, 0)),
            _const_spec(gmix.shape), _const_spec(whg.shape), _const_spec(loglb.shape),
            _const_spec(log1mlb.shape), _const_spec(gout.shape), _const_spec(tri.shape),
            _const_spec(cmask.shape),
        ],
        out_specs=pl.BlockSpec((tm, HG_WIDTH), lambda b, i: (b * nt + i, 0)),
        out_shape=jax.ShapeDtypeStruct((batch * seq, HG_WIDTH), BF16),
        scratch_shapes=[
            pltpu.VMEM((HG_HEADS, HG_DV, HG_DK), F32),
            pltpu.VMEM((HG_HEADS, HG_DV, HG_DK), F32),
            pltpu.VMEM((tm, HG_KW), F32),
            pltpu.VMEM((tm, HG_KW), F32),
            pltpu.VMEM((tm, HG_WIDTH), F32),
            pltpu.VMEM((tm, HG_KW), F32),
            pltpu.VMEM((tm, HG_WIDTH), F32),
        ],
        compiler_params=_params(("parallel", "arbitrary")),
        name="hgrn",
    )(x2, gmix, whg, loglb, log1mlb, gout, tri, cmask)


def _merge_kernel(x_ref, ya_ref, yb_ref, gmix_ref, wg_ref, wpa_ref, wpb_ref, wout_ref, o_ref):
    x = x_ref[...]
    h = _rms(x, gmix_ref[...]).astype(BF16)
    ya, yb = ya_ref[...], yb_ref[...]
    acc = x
    w = D_MODEL // MERGE_SPLIT
    for j in range(MERGE_SPLIT):
        cols = slice(j * w, (j + 1) * w)
        g_a = _dot(h, wg_ref[:, cols])
        g_b = _dot(h, wg_ref[:, D_MODEL + j * w:D_MODEL + (j + 1) * w])
        y = (jax.nn.sigmoid(g_a) * _dot(ya, wpa_ref[:, cols])
             + jax.nn.sigmoid(g_b) * _dot(yb, wpb_ref[:, cols]))
        acc = acc + _dot(y.astype(BF16), wout_ref[cols, :])
    o_ref[...] = acc


def _merge(x2, ya, yb, gmix, wg, wpa, wpb, wout, tm):
    n = x2.shape[0]
    return pl.pallas_call(
        _merge_kernel,
        grid=(n // tm,),
        in_specs=[
            pl.BlockSpec((tm, D_MODEL), lambda i: (i, 0)),
            pl.BlockSpec((tm, ya.shape[1]), lambda i: (i, 0)),
            pl.BlockSpec((tm, yb.shape[1]), lambda i: (i, 0)),
            _const_spec(gmix.shape), _const_spec(wg.shape), _const_spec(wpa.shape),
            _const_spec(wpb.shape), _const_spec(wout.shape),
        ],
        out_specs=pl.BlockSpec((tm, D_MODEL), lambda i: (i, 0)),
        out_shape=jax.ShapeDtypeStruct((n, D_MODEL), F32),
        compiler_params=_params(("parallel",)),
        name="merge",
    )(x2, ya, yb, gmix, wg, wpa, wpb, wout)


def _ffn_kernel(x_ref, gffn_ref, wgate_ref, wup_ref, wdown_ref, o_ref, *, n_split):
    x = x_ref[...]
    h = _rms(x, gffn_ref[...]).astype(BF16)
    d_ff = wgate_ref.shape[1]
    w = d_ff // n_split
    acc = x
    for j in range(n_split):
        gate = _dot(h, wgate_ref[:, j * w:(j + 1) * w])
        up = _dot(h, wup_ref[:, j * w:(j + 1) * w])
        act = (gate * jax.nn.sigmoid(gate) * up).astype(BF16)
        acc = acc + _dot(act, wdown_ref[j * w:(j + 1) * w, :])
    o_ref[...] = acc


def _ffn(x2, gffn, wgate, wup, wdown, tm, n_split):
    n = x2.shape[0]
    return pl.pallas_call(
        functools.partial(_ffn_kernel, n_split=n_split),
        grid=(n // tm,),
        in_specs=[
            pl.BlockSpec((tm, D_MODEL), lambda i: (i, 0)),
            _const_spec(gffn.shape), _const_spec(wgate.shape), _const_spec(wup.shape),
            _const_spec(wdown.shape),
        ],
        out_specs=pl.BlockSpec((tm, D_MODEL), lambda i: (i, 0)),
        out_shape=jax.ShapeDtypeStruct((n, D_MODEL), F32),
        compiler_params=_params(("parallel",)),
        name="ffn",
    )(x2, gffn, wgate, wup, wdown)


def _rope_tables(seq, gain, scale):
    pos = jnp.arange(seq, dtype=F32)
    inv_freq = 1.0 / (ROPE_THETA ** (jnp.arange(0, MLA_ROPE, 2, dtype=F32) / MLA_ROPE))
    ang = pos[:, None] * inv_freq[None, :]
    cos, sin = jnp.cos(ang), jnp.sin(ang)
    half = MLA_ROPE // 2
    g_nope, g_rope = gain[:MLA_NOPE], gain[MLA_NOPE:]
    nope = jnp.broadcast_to(g_nope[None, :], (seq, MLA_NOPE))
    c_rope = jnp.concatenate([cos, cos], axis=1) * g_rope[None, :]
    g_partner = jnp.concatenate([g_rope[half:], g_rope[:half]])
    s_rope = jnp.concatenate([-sin, sin], axis=1) * g_partner[None, :]
    zeros = jnp.zeros((seq, MLA_NOPE), F32)
    q_tab = jnp.concatenate([nope, c_rope, s_rope], axis=1) * scale
    ctab = jnp.concatenate([nope, c_rope, c_rope], axis=1) * scale
    stab = jnp.concatenate([zeros, s_rope, s_rope], axis=1) * scale
    return q_tab, ctab, stab


def _layer_params(l, seq, tm_hg, w_in, mla_w_uq, mla_w_ukv, mla_q_norm, mla_k_norm, lower_bounds):
    o = [0]
    for s in IN_SPLITS:
        o.append(o[-1] + s)
    wi = w_in[l]
    half = MLA_ROPE // 2
    z64 = jnp.zeros((D_MODEL, MLA_NOPE), F32)
    w_kpe = wi[:, o[2]:o[3]]
    w_kpe_sw = jnp.concatenate([w_kpe[:, half:], w_kpe[:, :half]], axis=1)
    wmla = jnp.concatenate([wi[:, o[0]:o[2]], z64, w_kpe, w_kpe, z64, w_kpe_sw, w_kpe_sw], axis=1).astype(BF16)
    whg = wi[:, o[3]:o[7]].astype(BF16)
    wgates = wi[:, o[7]:o[9]].astype(BF16)

    uq = mla_w_uq[l].reshape(Q_LORA, MLA_HEADS, MLA_QK)
    uq_nope, uq_rope = uq[..., :MLA_NOPE], uq[..., MLA_NOPE:]
    uq_rope_sw = jnp.concatenate([uq_rope[..., half:], uq_rope[..., :half]], axis=-1)
    wuq = jnp.concatenate([uq_nope, uq_rope, uq_rope_sw], axis=-1).reshape(
        Q_LORA, MLA_HEADS * HEAD_PAD).astype(BF16)

    ukv = mla_w_ukv[l].reshape(KV_LORA, MLA_HEADS, MLA_NOPE + MLA_V)
    zk64 = jnp.zeros((KV_LORA, MLA_HEADS, HEAD_PAD - MLA_NOPE), F32)
    wuk = jnp.concatenate([ukv[..., :MLA_NOPE], zk64], axis=-1).reshape(KV_LORA, MLA_HEADS * HEAD_PAD)
    wuv = ukv[..., MLA_NOPE:].reshape(KV_LORA, MLA_HEADS * MLA_V)
    wukv = jnp.concatenate([wuk, wuv], axis=1).astype(BF16)

    tq_tab, _, _ = _rope_tables(seq, mla_q_norm[l], MLA_QK ** -0.5 * LOG2_E)
    _, ck, sk = _rope_tables(seq, mla_k_norm[l], 1.0)
    norm_mask = (jnp.arange(HEAD_PAD) < MLA_QK).astype(F32)[None, :]

    lb = lower_bounds[l][None, :]
    r = jnp.arange(tm_hg)
    tri = ((r[:, None] // HG_CHUNK == r[None, :] // HG_CHUNK) & (r[:, None] >= r[None, :])).astype(BF16)
    nc = tm_hg // HG_CHUNK
    cmask = (jnp.arange(nc * HG_DV)[:, None] // HG_DV == r[None, :] // HG_CHUNK).astype(BF16)
    return dict(wmla=wmla, whg=whg, wgates=wgates, wuq=wuq, wukv=wukv, tabs=(tq_tab, ck, sk), norm_mask=norm_mask,
                loglb=jnp.log(lb), log1mlb=jnp.log1p(-lb), tri=tri, cmask=cmask)


def kernel(x, norm_mix, w_in, mla_norm_cq, mla_w_uq, mla_norm_ckv, mla_w_ukv, mla_q_norm, mla_k_norm,
           hg_lb_logits, hg_out_norm, w_proj_a, w_proj_b, w_out, norm_ffn, w_gate, w_up, w_down):
    batch, seq, d = x.shape
    assert d == D_MODEL
    depth = w_in.shape[0]
    tm = min(256, seq)
    tm_ffn = min(1024, seq)
    assert seq % tm == 0 and seq % tm_ffn == 0 and tm % HG_CHUNK == 0

    p = jax.nn.softmax(hg_lb_logits.astype(F32), axis=0)
    lower_bounds = jnp.maximum(jnp.cumsum(p, axis=0) - p[0:1], 0.0)

    x2 = x.reshape(batch * seq, d)
    for l in range(depth):
        lp = _layer_params(l, seq, tm, w_in, mla_w_uq, mla_w_ukv, mla_q_norm, mla_k_norm, lower_bounds)
        gmix = norm_mix[l][None, :]
        qt, k, vt = _mla_prep(x2, gmix, lp["wmla"], mla_norm_cq[l][None, :], lp["wuq"],
                             mla_norm_ckv[l][None, :], lp["wukv"], lp["norm_mask"], lp["tabs"], seq, tm)
        y_a = _mla_attn(qt, k, vt, batch, seq, tm)
        y_b = _hgrn(x2, gmix, lp["whg"], lp["loglb"], lp["log1mlb"], hg_out_norm[l][None, :],
                    lp["tri"], lp["cmask"], batch, seq, tm)
        x2 = _merge(x2, y_a, y_b, gmix, lp["wgates"], w_proj_a[l].astype(BF16),
                    w_proj_b[l].astype(BF16), w_out[l].astype(BF16), tm_ffn)
        x2 = _ffn(x2, norm_ffn[l][None, :], w_gate[l].astype(BF16), w_up[l].astype(BF16),
                  w_down[l].astype(BF16), tm_ffn, FFN_SPLIT)
    return x2.reshape(batch, seq, d)
```

```python
import functools

import jax
import jax.numpy as jnp
from jax import lax
from jax.experimental import pallas as pl
from jax.experimental.pallas import tpu as pltpu

D_MODEL = 1024
MLA_HEADS = 8
MLA_NOPE = 64
MLA_ROPE = 32
MLA_QK = MLA_NOPE + MLA_ROPE
MLA_V = 64
Q_LORA = 384
KV_LORA = 256
ROPE_THETA = 10000.0
HG_HEADS = 4
HG_DK = 128
HG_DV = 128
HG_KW = HG_HEADS * HG_DK
HG_WIDTH = HG_HEADS * HG_DV
NORM_EPS = 1e-6
IN_SPLITS = (Q_LORA, KV_LORA, MLA_ROPE, HG_KW, HG_KW, HG_WIDTH, HG_WIDTH, D_MODEL, D_MODEL)

LANES = 128
HEAD_PAD = LANES
ATTN_GROUP = 4
HG_CHUNK = 32
HG_SUB = 256
HG_SAFE_DECAY = 60.0
LOG2_E = 1.4426950408889634
MERGE_SPLIT = 2
FFN_SPLIT = 11
VMEM_LIMIT = 56 * 1024 * 1024

BF16 = jnp.bfloat16
F32 = jnp.float32
NT = (((1,), (1,)), ((), ()))
TN = (((0,), (0,)), ((), ()))


def _rms(x, g):
    return x * lax.rsqrt(jnp.mean(x * x, axis=-1, keepdims=True) + NORM_EPS) * g


def _dot(a, b):
    return jnp.dot(a, b, preferred_element_type=F32)


def _const_spec(shape):
    nd = len(shape)
    return pl.BlockSpec(shape, lambda *_: (0,) * nd, pipeline_mode=pl.Buffered(1))


def _params(sem):
    return pltpu.CompilerParams(dimension_semantics=sem, vmem_limit_bytes=VMEM_LIMIT)


def _mla_prep_kernel(x_ref, gmix_ref, wmla_ref, gcq_ref, wuq_ref, gckv_ref, wukv_ref, nmask_ref,
                     tq_ref, ck_ref, sk_ref, qt_ref, k_ref, vt_ref, *, sub):
    n_sub = x_ref.shape[0] // sub
    gmix, nmask = gmix_ref[...], nmask_ref[...]
    inv_qk = 1.0 / MLA_QK
    gw = ATTN_GROUP * MLA_V
    zcs = [_dot(_rms(x_ref[u * sub:(u + 1) * sub, :], gmix).astype(BF16), wmla_ref[...]) for u in range(n_sub)]
    for u in range(n_sub):
        rows = slice(u * sub, (u + 1) * sub)
        zc = zcs[u]
        c_q = zc[:, :Q_LORA]
        c_kv = zc[:, Q_LORA:Q_LORA + KV_LORA]
        kpe_a = zc[:, 640:768]
        kpe_b = zc[:, 768:896]
        q2 = _dot(_rms(c_q, gcq_ref[...]).astype(BF16), wuq_ref[...])
        kv2 = _dot(_rms(c_kv, gckv_ref[...]).astype(BF16), wukv_ref[...])
        tq_tab, ck, sk = tq_ref[rows, :], ck_ref[rows, :], sk_ref[rows, :]
        for hd in range(MLA_HEADS):
            lo = hd * HEAD_PAD
            qa = q2[:, lo:lo + HEAD_PAD]
            rq = lax.rsqrt(jnp.sum(qa * qa * nmask, axis=-1, keepdims=True) * inv_qk + NORM_EPS)
            qt_ref[u, lo:lo + HEAD_PAD, :] = (qa * tq_tab * rq).astype(BF16).T
            ka = kv2[:, lo:lo + HEAD_PAD] + kpe_a
            rk = lax.rsqrt(jnp.sum(ka * ka * nmask, axis=-1, keepdims=True) * inv_qk + NORM_EPS)
            k_ref[rows, lo:lo + HEAD_PAD] = ((ka * ck + kpe_b * sk) * rk).astype(BF16)
        v = kv2[:, MLA_HEADS * HEAD_PAD:]
        for g in range(MLA_HEADS // ATTN_GROUP):
            vt_ref[g, u] = v[:, g * gw:(g + 1) * gw].astype(BF16).T


def _mla_prep(x2, gmix, wmla, gcq, wuq, gckv, wukv, nmask, tabs, seq, tm, sub):
    n = x2.shape[0]
    nt = seq // tm
    n_sub = tm // sub
    batch = n // seq
    tab_spec = pl.BlockSpec((tm, LANES), lambda i: (i % nt, 0))
    return pl.pallas_call(
        functools.partial(_mla_prep_kernel, sub=sub),
        grid=(n // tm,),
        in_specs=[
            pl.BlockSpec((tm, D_MODEL), lambda i: (i, 0)),
            _const_spec(gmix.shape), _const_spec(wmla.shape), _const_spec(gcq.shape),
            _const_spec(wuq.shape), _const_spec(gckv.shape), _const_spec(wukv.shape),
            _const_spec(nmask.shape), tab_spec, tab_spec, tab_spec,
        ],
        out_specs=[
            pl.BlockSpec((None, n_sub, MLA_HEADS * HEAD_PAD, sub), lambda i: (i // nt, i % nt, 0, 0)),
            pl.BlockSpec((tm, MLA_HEADS * HEAD_PAD), lambda i: (i, 0)),
            pl.BlockSpec((None, MLA_HEADS // ATTN_GROUP, n_sub, ATTN_GROUP * MLA_V, sub),
                         lambda i: (i // nt, 0, i % nt, 0, 0)),
        ],
        out_shape=[
            jax.ShapeDtypeStruct((batch, seq // sub, MLA_HEADS * HEAD_PAD, sub), BF16),
            jax.ShapeDtypeStruct((n, MLA_HEADS * HEAD_PAD), BF16),
            jax.ShapeDtypeStruct((batch, MLA_HEADS // ATTN_GROUP, seq // sub, ATTN_GROUP * MLA_V, sub), BF16),
        ],
        compiler_params=_params(("parallel",)),
        name="mla_prep",
    )(x2, gmix, wmla, gcq, wuq, gckv, wukv, nmask, *tabs)


def _attn_kernel(qt_ref, k_ref, vt_ref, o_ref, s0_scr, m0_scr, *, tq):
    i = pl.program_id(2)
    nq = pl.num_programs(2)
    kv_idx = lax.broadcasted_iota(jnp.int32, (tq, tq), 0)
    q_idx = lax.broadcasted_iota(jnp.int32, (tq, tq), 1)
    causal = q_idx >= kv_idx

    def scores(j, iq):
        start = pl.multiple_of(j * tq, tq)
        out = []
        for hh in range(ATTN_GROUP):
            sc = _dot(k_ref[pl.ds(start, tq), hh * HEAD_PAD:(hh + 1) * HEAD_PAD],
                      qt_ref[iq, hh * HEAD_PAD:(hh + 1) * HEAD_PAD, :])
            out.append((sc, jnp.max(sc, axis=0, keepdims=True)))
        return tuple(out)

    def update(j, s_all, carry, masked):
        vt = vt_ref[j]
        out = []
        for hh in range(ATTN_GROUP):
            m, l, acc = carry[hh]
            s, s_max = s_all[hh]
            if masked:
                s = jnp.where(causal, s, -jnp.inf)
                s_max = jnp.max(s, axis=0, keepdims=True)
            m_new = jnp.maximum(m, s_max)
            alpha = jnp.exp2(m - m_new)
            p = jnp.exp2(s - m_new)
            l = alpha * l + jnp.sum(p, axis=0, keepdims=True)
            acc = alpha * acc + _dot(vt[hh * MLA_V:(hh + 1) * MLA_V, :], p.astype(BF16))
            out.append((m_new, l, acc))
        return tuple(out)

    def stash(s_all):
        for hh in range(ATTN_GROUP):
            s0_scr[hh] = s_all[hh][0]
            m0_scr[hh] = s_all[hh][1]

    @pl.when(i == 0)
    def _():
        stash(scores(0, 0))

    first = tuple((s0_scr[hh], m0_scr[hh]) for hh in range(ATTN_GROUP))
    init = tuple((jnp.full((1, tq), -jnp.inf, F32), jnp.zeros((1, tq), F32), jnp.zeros((MLA_V, tq), F32))
                 for _ in range(ATTN_GROUP))

    def body(j, c):
        s_next = scores(j + 1, i)
        return s_next, update(j, c[0], c[1], False)

    s_last, carry = lax.fori_loop(0, i, body, (first, init))
    s_ahead = scores(0, jnp.minimum(i + 1, nq - 1))
    carry = update(i, s_last, carry, True)
    stash(s_ahead)
    o_t = jnp.concatenate([acc / l for _, l, acc in carry], axis=0)
    o_ref[...] = o_t.T.astype(BF16)


def _mla_attn(qt, k, vt, batch, seq, tq):
    nq = seq // tq
    return pl.pallas_call(
        functools.partial(_attn_kernel, tq=tq),
        grid=(batch, MLA_HEADS // ATTN_GROUP, nq),
        in_specs=[
            pl.BlockSpec((None, nq, ATTN_GROUP * HEAD_PAD, tq), lambda b, g, i: (b, 0, g, 0)),
            pl.BlockSpec((seq, ATTN_GROUP * HEAD_PAD), lambda b, g, i: (b, g)),
            pl.BlockSpec((None, None, nq, ATTN_GROUP * MLA_V, tq), lambda b, g, i: (b, g, 0, 0, 0)),
        ],
        out_specs=pl.BlockSpec((tq, ATTN_GROUP * MLA_V), lambda b, g, i: (b * nq + i, g)),
        out_shape=jax.ShapeDtypeStruct((batch * seq, MLA_HEADS * MLA_V), BF16),
        scratch_shapes=[
            pltpu.VMEM((ATTN_GROUP, tq, tq), F32),
            pltpu.VMEM((ATTN_GROUP, 1, tq), F32),
        ],
        compiler_params=_params(("parallel", "parallel", "arbitrary")),
        name="mla_attn",
    )(qt, k, vt)


def _split3(x):
    a = x.astype(BF16)
    r = x - a.astype(F32)
    b = r.astype(BF16)
    c = (r - b.astype(F32)).astype(BF16)
    return a, b, c


def _hgrn_kernel(x_ref, gmix_ref, whg_ref, loglb_ref, log1mlb_ref, gout_ref, tri_ref, cmask_ref, o_ref,
                 st_ref, stbak_ref, q_scr, k_scr, v_scr, b_scr, o_scr, g_scr, *, tm):
    S = HG_SUB
    C = HG_CHUNK
    n_sub = tm // S
    ncs = S // C
    nc = tm // C

    @pl.when(pl.program_id(1) == 0)
    def _():
        st_ref[...] = jnp.zeros_like(st_ref)

    stbak_ref[...] = st_ref[...]
    gmix = gmix_ref[...]
    zs = [_dot(_rms(x_ref[u * S:(u + 1) * S, :], gmix).astype(BF16), whg_ref[...]) for u in range(n_sub)]

    tri = tri_ref[...]
    cmask = cmask_ref[...]
    row = lax.broadcasted_iota(jnp.int32, (S, S), 0)
    col = lax.broadcasted_iota(jnp.int32, (S, S), 1)
    causal = (row >= col) & (row // C == col // C)
    a = loglb_ref[...]
    log1mlb = log1mlb_ref[...]
    sts = [st_ref[hd] for hd in range(HG_HEADS)]
    b_mins = []
    for u in range(n_sub):
        rows = slice(u * S, (u + 1) * S)
        z = zs[u]
        zq = z[:, :HG_KW]
        zf = z[:, HG_KW:2 * HG_KW]
        zi = z[:, 2 * HG_KW:2 * HG_KW + HG_WIDTH]
        zg = z[:, 2 * HG_KW + HG_WIDTH:]
        log_sig = jnp.minimum(zf, 0.0) - jnp.log(1.0 + jnp.exp(-jnp.abs(zf)))
        c = log1mlb + log_sig
        log_f = jnp.maximum(a, c) + jnp.log(1.0 + jnp.exp(-jnp.abs(a - c)))
        kk = 1.0 - jnp.exp(log_f)
        l1, l2, l3 = _split3(log_f)
        b = _dot(tri, l1) + _dot(tri, l2) + _dot(tri, l3)
        b3 = b.reshape(ncs, C, HG_KW)
        b_last = b3[:, C - 1:C, :]
        q_scr[rows, :] = zq
        k_scr[rows, :] = kk
        v_scr[rows, :] = zi
        b_scr[rows, :] = b
        g_scr[rows, :] = zg * jax.nn.sigmoid(zg)
        b_mins.append(jnp.min(b))

        qe = (zq * jnp.exp(b)).astype(BF16)
        ke = (kk * jnp.exp(jnp.minimum(-b, HG_SAFE_DECAY + 20.0))).astype(BF16)
        kl = (kk.reshape(ncs, C, HG_KW) * jnp.exp(b_last - b3)).reshape(S, HG_KW).astype(BF16)
        e_last = jnp.exp(b_last)
        vb = zi.astype(BF16)
        vt = vb.T
        for hd in range(HG_HEADS):
            lo = hd * HG_DK
            qe_h = qe[:, lo:lo + HG_DK]
            att = lax.dot_general(qe_h, ke[:, lo:lo + HG_DK], NT, preferred_element_type=F32)
            att = jnp.where(causal, att, 0.0).astype(BF16)
            o_intra = _dot(att, vb[:, lo:lo + HG_DV])
            vt_h = vt[lo:lo + HG_DV, :]
            upd = _dot(jnp.concatenate([vt_h] * ncs, axis=0) * cmask, kl[:, lo:lo + HG_DK])
            st = sts[hd]
            for ci in range(ncs):
                r0 = ci * C
                o_inter = lax.dot_general(qe_h[r0:r0 + C], st.astype(BF16), NT, preferred_element_type=F32)
                o_scr[u * S + r0:u * S + r0 + C, lo:lo + HG_DV] = o_intra[r0:r0 + C] + o_inter
                st = st * e_last[ci, :, lo:lo + HG_DK] + upd[ci * HG_DV:(ci + 1) * HG_DV]
            sts[hd] = st
    for hd in range(HG_HEADS):
        st_ref[hd] = sts[hd]
    b_min = b_mins[0]
    for u in range(1, n_sub):
        b_min = jnp.minimum(b_min, b_mins[u])
    unsafe = b_min < -HG_SAFE_DECAY

    @pl.when(unsafe)
    def _():
        st_ref[...] = stbak_ref[...]
        t_idx = lax.broadcasted_iota(jnp.int32, (C, 1), 0)
        for hd in range(HG_HEADS):
            lo = hd * HG_DK

            def chunk_body(ci, _, lo=lo, hd=hd):
                r0 = pl.multiple_of(ci * C, C)
                q_c = q_scr[pl.ds(r0, C), lo:lo + HG_DK]
                k_c = k_scr[pl.ds(r0, C), lo:lo + HG_DK]
                v_c = v_scr[pl.ds(r0, C), lo:lo + HG_DV]
                b_c = b_scr[pl.ds(r0, C), lo:lo + HG_DK]
                bl = b_c[C - 1:C, :]

                def s_body(s, o_acc):
                    pick = t_idx == s
                    ks = jnp.sum(jnp.where(pick, k_c, 0.0), axis=0, keepdims=True)
                    bs = jnp.sum(jnp.where(pick, b_c, 0.0), axis=0, keepdims=True)
                    vs = jnp.sum(jnp.where(pick, v_c, 0.0), axis=0, keepdims=True)
                    dec = jnp.exp(jnp.minimum(b_c - bs, 0.0))
                    w = jnp.sum(q_c * ks * dec, axis=-1, keepdims=True)
                    w = jnp.where(t_idx >= s, w, 0.0)
                    return o_acc + w * vs

                o_intra = lax.fori_loop(0, C, s_body, jnp.zeros((C, HG_DV), F32))
                st = st_ref[hd]
                qe_c = (q_c * jnp.exp(b_c)).astype(BF16)
                o_inter = lax.dot_general(qe_c, st.astype(BF16), NT, preferred_element_type=F32)
                o_scr[pl.ds(r0, C), lo:lo + HG_DV] = o_intra + o_inter
                kl_c = (k_c * jnp.exp(bl - b_c)).astype(BF16)
                upd = lax.dot_general(v_c.astype(BF16), kl_c, TN, preferred_element_type=F32)
                st_ref[hd] = st * jnp.exp(bl) + upd
                return 0

            lax.fori_loop(0, nc, chunk_body, 0)

    o = o_scr[...]
    gate = g_scr[...]
    gout = gout_ref[...]
    for hd in range(HG_HEADS):
        lo = hd * HG_DV
        o_ref[:, lo:lo + HG_DV] = (_rms(o[:, lo:lo + HG_DV], gout) * gate[:, lo:lo + HG_DV]).astype(BF16)


def _hgrn(x2, gmix, whg, loglb, log1mlb, gout, tri, cmask, batch, seq, tm):
    nt = seq // tm
    return pl.pallas_call(
        functools.partial(_hgrn_kernel, tm=tm),
        grid=(batch, nt),
        in_specs=[
            pl.BlockSpec((tm, D_MODEL), lambda b, i: (b * nt + i, 0)),
            _const_spec(gmix.shape), _const_spec(whg.shape), _const_spec(loglb.shape),
            _const_spec(log1mlb.shape), _const_spec(gout.shape), _const_spec(tri.shape),
            _const_spec(cmask.shape),
        ],
        out_specs=pl.BlockSpec((tm, HG_WIDTH), lambda b, i: (b * nt + i, 0)),
        out_shape=jax.ShapeDtypeStruct((batch * seq, HG_WIDTH), BF16),
        scratch_shapes=[
            pltpu.VMEM((HG_HEADS, HG_DV, HG_DK), F32),
            pltpu.VMEM((HG_HEADS, HG_DV, HG_DK), F32),
            pltpu.VMEM((tm, HG_KW), F32),
            pltpu.VMEM((tm, HG_KW), F32),
            pltpu.VMEM((tm, HG_WIDTH), F32),
            pltpu.VMEM((tm, HG_KW), F32),
            pltpu.VMEM((tm, HG_WIDTH), F32),
            pltpu.VMEM((tm, HG_WIDTH), F32),
        ],
        compiler_params=_params(("parallel", "arbitrary")),
        name="hgrn",
    )(x2, gmix, whg, loglb, log1mlb, gout, tri, cmask)


def _merge_kernel(x_ref, ya_ref, yb_ref, gmix_ref, wg_ref, wpa_ref, wpb_ref, wout_ref, o_ref):
    x = x_ref[...]
    h = _rms(x, gmix_ref[...]).astype(BF16)
    ya, yb = ya_ref[...], yb_ref[...]
    acc = x
    w = D_MODEL // MERGE_SPLIT
    for j in range(MERGE_SPLIT):
        cols = slice(j * w, (j + 1) * w)
        g_a = _dot(h, wg_ref[:, cols])
        g_b = _dot(h, wg_ref[:, D_MODEL + j * w:D_MODEL + (j + 1) * w])
        y = (jax.nn.sigmoid(g_a) * _dot(ya, wpa_ref[:, cols])
             + jax.nn.sigmoid(g_b) * _dot(yb, wpb_ref[:, cols]))
        acc = acc + _dot(y.astype(BF16), wout_ref[cols, :])
    o_ref[...] = acc


def _merge(x2, ya, yb, gmix, wg, wpa, wpb, wout, tm):
    n = x2.shape[0]
    return pl.pallas_call(
        _merge_kernel,
        grid=(n // tm,),
        in_specs=[
            pl.BlockSpec((tm, D_MODEL), lambda i: (i, 0)),
            pl.BlockSpec((tm, ya.shape[1]), lambda i: (i, 0)),
            pl.BlockSpec((tm, yb.shape[1]), lambda i: (i, 0)),
            _const_spec(gmix.shape), _const_spec(wg.shape), _const_spec(wpa.shape),
            _const_spec(wpb.shape), _const_spec(wout.shape),
        ],
        out_specs=pl.BlockSpec((tm, D_MODEL), lambda i: (i, 0)),
        out_shape=jax.ShapeDtypeStruct((n, D_MODEL), F32),
        compiler_params=_params(("parallel",)),
        name="merge",
    )(x2, ya, yb, gmix, wg, wpa, wpb, wout)


def _ffn_kernel(x_ref, gffn_ref, wgate_ref, wup_ref, wdown_ref, o_ref, *, n_split):
    x = x_ref[...]
    h = _rms(x, gffn_ref[...]).astype(BF16)
    d_ff = wgate_ref.shape[1]
    w = d_ff // n_split
    acc = x
    for j in range(n_split):
        gate = _dot(h, wgate_ref[:, j * w:(j + 1) * w])
        up = _dot(h, wup_ref[:, j * w:(j + 1) * w])
        act = (gate * jax.nn.sigmoid(gate) * up).astype(BF16)
        acc = acc + _dot(act, wdown_ref[j * w:(j + 1) * w, :])
    o_ref[...] = acc


def _ffn(x2, gffn, wgate, wup, wdown, tm, n_split):
    n = x2.shape[0]
    return pl.pallas_call(
        functools.partial(_ffn_kernel, n_split=n_split),
        grid=(n // tm,),
        in_specs=[
            pl.BlockSpec((tm, D_MODEL), lambda i: (i, 0)),
            _const_spec(gffn.shape), _const_spec(wgate.shape), _const_spec(wup.shape),
            _const_spec(wdown.shape),
        ],
        out_specs=pl.BlockSpec((tm, D_MODEL), lambda i: (i, 0)),
        out_shape=jax.ShapeDtypeStruct((n, D_MODEL), F32),
        compiler_params=_params(("parallel",)),
        name="ffn",
    )(x2, gffn, wgate, wup, wdown)


def _rope_tables(seq, gain, scale):
    pos = jnp.arange(seq, dtype=F32)
    inv_freq = 1.0 / (ROPE_THETA ** (jnp.arange(0, MLA_ROPE, 2, dtype=F32) / MLA_ROPE))
    ang = pos[:, None] * inv_freq[None, :]
    cos, sin = jnp.cos(ang), jnp.sin(ang)
    half = MLA_ROPE // 2
    g_nope, g_rope = gain[:MLA_NOPE], gain[MLA_NOPE:]
    nope = jnp.broadcast_to(g_nope[None, :], (seq, MLA_NOPE))
    c_rope = jnp.concatenate([cos, cos], axis=1) * g_rope[None, :]
    g_partner = jnp.concatenate([g_rope[half:], g_rope[:half]])
    s_rope = jnp.concatenate([-sin, sin], axis=1) * g_partner[None, :]
    zeros = jnp.zeros((seq, MLA_NOPE), F32)
    q_tab = jnp.concatenate([nope, c_rope, s_rope], axis=1) * scale
    ctab = jnp.concatenate([nope, c_rope, c_rope], axis=1) * scale
    stab = jnp.concatenate([zeros, s_rope, s_rope], axis=1) * scale
    return q_tab, ctab, stab


def _layer_params(l, seq, tm_hg, w_in, mla_w_uq, mla_w_ukv, mla_q_norm, mla_k_norm, lower_bounds):
    o = [0]
    for s in IN_SPLITS:
        o.append(o[-1] + s)
    wi = w_in[l]
    half = MLA_ROPE // 2
    z64 = jnp.zeros((D_MODEL, MLA_NOPE), F32)
    w_kpe = wi[:, o[2]:o[3]]
    w_kpe_sw = jnp.concatenate([w_kpe[:, half:], w_kpe[:, :half]], axis=1)
    wmla = jnp.concatenate([wi[:, o[0]:o[2]], z64, w_kpe, w_kpe, z64, w_kpe_sw, w_kpe_sw], axis=1).astype(BF16)
    whg = wi[:, o[3]:o[7]].astype(BF16)
    wgates = wi[:, o[7]:o[9]].astype(BF16)

    uq = mla_w_uq[l].reshape(Q_LORA, MLA_HEADS, MLA_QK)
    uq_nope, uq_rope = uq[..., :MLA_NOPE], uq[..., MLA_NOPE:]
    uq_rope_sw = jnp.concatenate([uq_rope[..., half:], uq_rope[..., :half]], axis=-1)
    wuq = jnp.concatenate([uq_nope, uq_rope, uq_rope_sw], axis=-1).reshape(
        Q_LORA, MLA_HEADS * HEAD_PAD).astype(BF16)

    ukv = mla_w_ukv[l].reshape(KV_LORA, MLA_HEADS, MLA_NOPE + MLA_V)
    zk64 = jnp.zeros((KV_LORA, MLA_HEADS, HEAD_PAD - MLA_NOPE), F32)
    wuk = jnp.concatenate([ukv[..., :MLA_NOPE], zk64], axis=-1).reshape(KV_LORA, MLA_HEADS * HEAD_PAD)
    wuv = ukv[..., MLA_NOPE:].reshape(KV_LORA, MLA_HEADS * MLA_V)
    wukv = jnp.concatenate([wuk, wuv], axis=1).astype(BF16)

    tq_tab, _, _ = _rope_tables(seq, mla_q_norm[l], MLA_QK ** -0.5 * LOG2_E)
    _, ck, sk = _rope_tables(seq, mla_k_norm[l], 1.0)
    norm_mask = (jnp.arange(HEAD_PAD) < MLA_QK).astype(F32)[None, :]

    lb = lower_bounds[l][None, :]
    r = jnp.arange(tm_hg)
    tri = ((r[:, None] // HG_CHUNK == r[None, :] // HG_CHUNK) & (r[:, None] >= r[None, :])).astype(BF16)
    nc = tm_hg // HG_CHUNK
    cmask = (jnp.arange(nc * HG_DV)[:, None] // HG_DV == r[None, :] // HG_CHUNK).astype(BF16)
    return dict(wmla=wmla, whg=whg, wgates=wgates, wuq=wuq, wukv=wukv, tabs=(tq_tab, ck, sk), norm_mask=norm_mask,
                loglb=jnp.log(lb), log1mlb=jnp.log1p(-lb), tri=tri, cmask=cmask)


def kernel(x, norm_mix, w_in, mla_norm_cq, mla_w_uq, mla_norm_ckv, mla_w_ukv, mla_q_norm, mla_k_norm,
           hg_lb_logits, hg_out_norm, w_proj_a, w_proj_b, w_out, norm_ffn, w_gate, w_up, w_down):
    batch, seq, d = x.shape
    assert d == D_MODEL
    depth = w_in.shape[0]
    tm = min(256, seq)
    tm_prep = min(2 * tm, seq)
    tm_hg = min(2 * HG_SUB, seq)
    tm_ffn = min(1024, seq)
    assert all(seq % t == 0 for t in (tm, tm_prep, tm_hg, tm_ffn)) and tm % HG_CHUNK == 0

    p = jax.nn.softmax(hg_lb_logits.astype(F32), axis=0)
    lower_bounds = jnp.maximum(jnp.cumsum(p, axis=0) - p[0:1], 0.0)

    x2 = x.reshape(batch * seq, d)
    for l in range(depth):
        lp = _layer_params(l, seq, min(HG_SUB, seq), w_in, mla_w_uq, mla_w_ukv, mla_q_norm, mla_k_norm, lower_bounds)
        gmix = norm_mix[l][None, :]
        qt, k, vt = _mla_prep(x2, gmix, lp["wmla"], mla_norm_cq[l][None, :], lp["wuq"],
                             mla_norm_ckv[l][None, :], lp["wukv"], lp["norm_mask"], lp["tabs"], seq, tm_prep, tm)
        y_a = _mla_attn(qt, k, vt, batch, seq, tm)
        y_b = _hgrn(x2, gmix, lp["whg"], lp["loglb"], lp["log1mlb"], hg_out_norm[l][None, :],
                    lp["tri"], lp["cmask"], batch, seq, tm_hg)
        x2 = _merge(x2, y_a, y_b, gmix, lp["wgates"], w_proj_a[l].astype(BF16),
                    w_proj_b[l].astype(BF16), w_out[l].astype(BF16), tm_ffn)
        x2 = _ffn(x2, norm_ffn[l][None, :], w_gate[l].astype(BF16), w_up[l].astype(BF16),
                  w_down[l].astype(BF16), tm_ffn, FFN_SPLIT)
    return x2.reshape(batch, seq, d)
```

```python
import functools

import jax
import jax.numpy as jnp
from jax import lax
from jax.experimental import pallas as pl
from jax.experimental.pallas import tpu as pltpu

D_MODEL = 1024
MLA_HEADS = 8
MLA_NOPE = 64
MLA_ROPE = 32
MLA_QK = MLA_NOPE + MLA_ROPE
MLA_V = 64
Q_LORA = 384
KV_LORA = 256
ROPE_THETA = 10000.0
HG_HEADS = 4
HG_DK = 128
HG_DV = 128
HG_KW = HG_HEADS * HG_DK
HG_WIDTH = HG_HEADS * HG_DV
NORM_EPS = 1e-6
IN_SPLITS = (Q_LORA, KV_LORA, MLA_ROPE, HG_KW, HG_KW, HG_WIDTH, HG_WIDTH, D_MODEL, D_MODEL)

LANES = 128
HEAD_PAD = LANES
ATTN_GROUP = 4
HG_CHUNK = 32
HG_SUB = 256
HG_SAFE_DECAY = 60.0
LOG2_E = 1.4426950408889634
MERGE_SPLIT = 2
FFN_SPLIT = 11
VMEM_LIMIT = 56 * 1024 * 1024

BF16 = jnp.bfloat16
F32 = jnp.float32
NT = (((1,), (1,)), ((), ()))
TN = (((0,), (0,)), ((), ()))


def _rms(x, g):
    return x * lax.rsqrt(jnp.mean(x * x, axis=-1, keepdims=True) + NORM_EPS) * g


def _dot(a, b):
    return jnp.dot(a, b, preferred_element_type=F32)


def _const_spec(shape):
    nd = len(shape)
    return pl.BlockSpec(shape, lambda *_: (0,) * nd, pipeline_mode=pl.Buffered(1))


def _params(sem):
    return pltpu.CompilerParams(dimension_semantics=sem, vmem_limit_bytes=VMEM_LIMIT)


def _mla_prep_kernel(x_ref, gmix_ref, wmla_ref, gcq_ref, wuq_ref, gckv_ref, wukv_ref, nmask_ref,
                     tq_ref, ck_ref, sk_ref, qt_ref, k_ref, vt_ref, *, sub):
    n_sub = x_ref.shape[0] // sub
    gmix, nmask = gmix_ref[...], nmask_ref[...]
    inv_qk = 1.0 / MLA_QK
    gw = ATTN_GROUP * MLA_V
    zcs = [_dot(_rms(x_ref[u * sub:(u + 1) * sub, :], gmix).astype(BF16), wmla_ref[...]) for u in range(n_sub)]
    for u in range(n_sub):
        rows = slice(u * sub, (u + 1) * sub)
        zc = zcs[u]
        c_q = zc[:, :Q_LORA]
        c_kv = zc[:, Q_LORA:Q_LORA + KV_LORA]
        kpe_a = zc[:, 640:768]
        kpe_b = zc[:, 768:896]
        q2 = _dot(_rms(c_q, gcq_ref[...]).astype(BF16), wuq_ref[...])
        kv2 = _dot(_rms(c_kv, gckv_ref[...]).astype(BF16), wukv_ref[...])
        tq_tab, ck, sk = tq_ref[rows, :], ck_ref[rows, :], sk_ref[rows, :]
        for hd in range(MLA_HEADS):
            lo = hd * HEAD_PAD
            qa = q2[:, lo:lo + HEAD_PAD]
            rq = lax.rsqrt(jnp.sum(qa * qa * nmask, axis=-1, keepdims=True) * inv_qk + NORM_EPS)
            qt_ref[u, lo:lo + HEAD_PAD, :] = (qa * tq_tab * rq).astype(BF16).T
            ka = kv2[:, lo:lo + HEAD_PAD] + kpe_a
            rk = lax.rsqrt(jnp.sum(ka * ka * nmask, axis=-1, keepdims=True) * inv_qk + NORM_EPS)
            k_ref[rows, lo:lo + HEAD_PAD] = ((ka * ck + kpe_b * sk) * rk).astype(BF16)
        v = kv2[:, MLA_HEADS * HEAD_PAD:]
        for g in range(MLA_HEADS // ATTN_GROUP):
            vt_ref[g, u] = v[:, g * gw:(g + 1) * gw].astype(BF16).T


def _mla_prep(x2, gmix, wmla, gcq, wuq, gckv, wukv, nmask, tabs, seq, tm, sub):
    n = x2.shape[0]
    nt = seq // tm
    n_sub = tm // sub
    batch = n // seq
    tab_spec = pl.BlockSpec((tm, LANES), lambda i: (i % nt, 0))
    return pl.pallas_call(
        functools.partial(_mla_prep_kernel, sub=sub),
        grid=(n // tm,),
        in_specs=[
            pl.BlockSpec((tm, D_MODEL), lambda i: (i, 0)),
            _const_spec(gmix.shape), _const_spec(wmla.shape), _const_spec(gcq.shape),
            _const_spec(wuq.shape), _const_spec(gckv.shape), _const_spec(wukv.shape),
            _const_spec(nmask.shape), tab_spec, tab_spec, tab_spec,
        ],
        out_specs=[
            pl.BlockSpec((None, n_sub, MLA_HEADS * HEAD_PAD, sub), lambda i: (i // nt, i % nt, 0, 0)),
            pl.BlockSpec((tm, MLA_HEADS * HEAD_PAD), lambda i: (i, 0)),
            pl.BlockSpec((None, MLA_HEADS // ATTN_GROUP, n_sub, ATTN_GROUP * MLA_V, sub),
                         lambda i: (i // nt, 0, i % nt, 0, 0)),
        ],
        out_shape=[
            jax.ShapeDtypeStruct((batch, seq // sub, MLA_HEADS * HEAD_PAD, sub), BF16),
            jax.ShapeDtypeStruct((n, MLA_HEADS * HEAD_PAD), BF16),
            jax.ShapeDtypeStruct((batch, MLA_HEADS // ATTN_GROUP, seq // sub, ATTN_GROUP * MLA_V, sub), BF16),
        ],
        compiler_params=_params(("parallel",)),
        name="mla_prep",
    )(x2, gmix, wmla, gcq, wuq, gckv, wukv, nmask, *tabs)


def _attn_kernel(qt_ref, k_ref, vt_ref, o_ref, s_scr, m_scr, *, tq):
    i = pl.program_id(2)
    nq = pl.num_programs(2)
    kv_idx = lax.broadcasted_iota(jnp.int32, (tq, tq), 0)
    q_idx = lax.broadcasted_iota(jnp.int32, (tq, tq), 1)
    causal = q_idx >= kv_idx

    def scores(j, iq):
        start = pl.multiple_of(j * tq, tq)
        return tuple(_dot(k_ref[pl.ds(start, tq), hh * HEAD_PAD:(hh + 1) * HEAD_PAD],
                          qt_ref[iq, hh * HEAD_PAD:(hh + 1) * HEAD_PAD, :]) for hh in range(ATTN_GROUP))

    def stash(slot, s_all):
        for hh in range(ATTN_GROUP):
            s_scr[slot, hh] = s_all[hh]
            m_scr[slot, hh] = jnp.max(s_all[hh], axis=0, keepdims=True)

    def update(j, slot, carry, masked):
        vt = vt_ref[j]
        out = []
        for hh in range(ATTN_GROUP):
            m, l, acc = carry[hh]
            s = s_scr[slot, hh]
            if masked:
                s = jnp.where(causal, s, -jnp.inf)
                s_max = jnp.max(s, axis=0, keepdims=True)
            else:
                s_max = m_scr[slot, hh]
            m_new = jnp.maximum(m, s_max)
            alpha = jnp.exp2(m - m_new)
            p = jnp.exp2(s - m_new)
            l = alpha * l + jnp.sum(p, axis=0, keepdims=True)
            acc = alpha * acc + _dot(vt[hh * MLA_V:(hh + 1) * MLA_V, :], p.astype(BF16))
            out.append((m_new, l, acc))
        return tuple(out)

    @pl.when(i == 0)
    def _():
        stash(0, scores(0, 0))

    init = tuple((jnp.full((1, tq), -jnp.inf, F32), jnp.zeros((1, tq), F32), jnp.zeros((MLA_V, tq), F32))
                 for _ in range(ATTN_GROUP))

    def body(j, carry):
        s_next = scores(j + 1, i)
        carry = update(j, lax.rem(j, 2), carry, False)
        stash(lax.rem(j + 1, 2), s_next)
        return carry

    carry = lax.fori_loop(0, i, body, init)
    s_ahead = scores(0, jnp.minimum(i + 1, nq - 1))
    carry = update(i, lax.rem(i, 2), carry, True)
    stash(0, s_ahead)
    o_t = jnp.concatenate([acc / l for _, l, acc in carry], axis=0)
    o_ref[...] = o_t.T.astype(BF16)


def _mla_attn(qt, k, vt, batch, seq, tq):
    nq = seq // tq
    return pl.pallas_call(
        functools.partial(_attn_kernel, tq=tq),
        grid=(batch, MLA_HEADS // ATTN_GROUP, nq),
        in_specs=[
            pl.BlockSpec((None, nq, ATTN_GROUP * HEAD_PAD, tq), lambda b, g, i: (b, 0, g, 0)),
            pl.BlockSpec((seq, ATTN_GROUP * HEAD_PAD), lambda b, g, i: (b, g)),
            pl.BlockSpec((None, None, nq, ATTN_GROUP * MLA_V, tq), lambda b, g, i: (b, g, 0, 0, 0)),
        ],
        out_specs=pl.BlockSpec((tq, ATTN_GROUP * MLA_V), lambda b, g, i: (b * nq + i, g)),
        out_shape=jax.ShapeDtypeStruct((batch * seq, MLA_HEADS * MLA_V), BF16),
        scratch_shapes=[
            pltpu.VMEM((2, ATTN_GROUP, tq, tq), F32),
            pltpu.VMEM((2, ATTN_GROUP, 1, tq), F32),
        ],
        compiler_params=_params(("parallel", "parallel", "arbitrary")),
        name="mla_attn",
    )(qt, k, vt)


def _split3(x):
    a = x.astype(BF16)
    r = x - a.astype(F32)
    b = r.astype(BF16)
    c = (r - b.astype(F32)).astype(BF16)
    return a, b, c


def _hgrn_kernel(x_ref, gmix_ref, whg_ref, loglb_ref, log1mlb_ref, gout_ref, tri_ref, cmask_ref, o_ref,
                 st_ref, stbak_ref, q_scr, k_scr, v_scr, b_scr, o_scr, g_scr, *, tm):
    S = HG_SUB
    C = HG_CHUNK
    n_sub = tm // S
    ncs = S // C
    nc = tm // C

    @pl.when(pl.program_id(1) == 0)
    def _():
        st_ref[...] = jnp.zeros_like(st_ref)

    stbak_ref[...] = st_ref[...]
    gmix = gmix_ref[...]
    zs = [_dot(_rms(x_ref[u * S:(u + 1) * S, :], gmix).astype(BF16), whg_ref[...]) for u in range(n_sub)]

    tri = tri_ref[...]
    cmask = cmask_ref[...]
    row = lax.broadcasted_iota(jnp.int32, (S, S), 0)
    col = lax.broadcasted_iota(jnp.int32, (S, S), 1)
    causal = (row >= col) & (row // C == col // C)
    a = loglb_ref[...]
    log1mlb = log1mlb_ref[...]
    sts = [st_ref[hd] for hd in range(HG_HEADS)]
    b_mins = []
    for u in range(n_sub):
        rows = slice(u * S, (u + 1) * S)
        z = zs[u]
        zq = z[:, :HG_KW]
        zf = z[:, HG_KW:2 * HG_KW]
        zi = z[:, 2 * HG_KW:2 * HG_KW + HG_WIDTH]
        zg = z[:, 2 * HG_KW + HG_WIDTH:]
        log_sig = jnp.minimum(zf, 0.0) - jnp.log(1.0 + jnp.exp(-jnp.abs(zf)))
        c = log1mlb + log_sig
        log_f = jnp.maximum(a, c) + jnp.log(1.0 + jnp.exp(-jnp.abs(a - c)))
        kk = 1.0 - jnp.exp(log_f)
        l1, l2, l3 = _split3(log_f)
        b = _dot(tri, l1) + _dot(tri, l2) + _dot(tri, l3)
        b3 = b.reshape(ncs, C, HG_KW)
        b_last = b3[:, C - 1:C, :]
        q_scr[rows, :] = zq
        k_scr[rows, :] = kk
        v_scr[rows, :] = zi
        b_scr[rows, :] = b
        g_scr[rows, :] = zg * jax.nn.sigmoid(zg)
        b_mins.append(jnp.min(b))

        qe = (zq * jnp.exp(b)).astype(BF16)
        ke = (kk * jnp.exp(jnp.minimum(-b, HG_SAFE_DECAY + 20.0))).astype(BF16)
        kl = (kk.reshape(ncs, C, HG_KW) * jnp.exp(b_last - b3)).reshape(S, HG_KW).astype(BF16)
        e_last = jnp.exp(b_last)
        vb = zi.astype(BF16)
        vt = vb.T
        for hd in range(HG_HEADS):
            lo = hd * HG_DK
            qe_h = qe[:, lo:lo + HG_DK]
            att = lax.dot_general(qe_h, ke[:, lo:lo + HG_DK], NT, preferred_element_type=F32)
            att = jnp.where(causal, att, 0.0).astype(BF16)
            o_intra = _dot(att, vb[:, lo:lo + HG_DV])
            vt_h = vt[lo:lo + HG_DV, :]
            upd = _dot(jnp.concatenate([vt_h] * ncs, axis=0) * cmask, kl[:, lo:lo + HG_DK])
            st = sts[hd]
            for ci in range(ncs):
                r0 = ci * C
                o_inter = lax.dot_general(qe_h[r0:r0 + C], st.astype(BF16), NT, preferred_element_type=F32)
                o_scr[u * S + r0:u * S + r0 + C, lo:lo + HG_DV] = o_intra[r0:r0 + C] + o_inter
                st = st * e_last[ci, :, lo:lo + HG_DK] + upd[ci * HG_DV:(ci + 1) * HG_DV]
            sts[hd] = st
    for hd in range(HG_HEADS):
        st_ref[hd] = sts[hd]
    b_min = b_mins[0]
    for u in range(1, n_sub):
        b_min = jnp.minimum(b_min, b_mins[u])
    unsafe = b_min < -HG_SAFE_DECAY

    @pl.when(unsafe)
    def _():
        st_ref[...] = stbak_ref[...]
        t_idx = lax.broadcasted_iota(jnp.int32, (C, 1), 0)
        for hd in range(HG_HEADS):
            lo = hd * HG_DK

            def chunk_body(ci, _, lo=lo, hd=hd):
                r0 = pl.multiple_of(ci * C, C)
                q_c = q_scr[pl.ds(r0, C), lo:lo + HG_DK]
                k_c = k_scr[pl.ds(r0, C), lo:lo + HG_DK]
                v_c = v_scr[pl.ds(r0, C), lo:lo + HG_DV]
                b_c = b_scr[pl.ds(r0, C), lo:lo + HG_DK]
                bl = b_c[C - 1:C, :]

                def s_body(s, o_acc):
                    pick = t_idx == s
                    ks = jnp.sum(jnp.where(pick, k_c, 0.0), axis=0, keepdims=True)
                    bs = jnp.sum(jnp.where(pick, b_c, 0.0), axis=0, keepdims=True)
                    vs = jnp.sum(jnp.where(pick, v_c, 0.0), axis=0, keepdims=True)
                    dec = jnp.exp(jnp.minimum(b_c - bs, 0.0))
                    w = jnp.sum(q_c * ks * dec, axis=-1, keepdims=True)
                    w = jnp.where(t_idx >= s, w, 0.0)
                    return o_acc + w * vs

                o_intra = lax.fori_loop(0, C, s_body, jnp.zeros((C, HG_DV), F32))
                st = st_ref[hd]
                qe_c = (q_c * jnp.exp(b_c)).astype(BF16)
                o_inter = lax.dot_general(qe_c, st.astype(BF16), NT, preferred_element_type=F32)
                o_scr[pl.ds(r0, C), lo:lo + HG_DV] = o_intra + o_inter
                kl_c = (k_c * jnp.exp(bl - b_c)).astype(BF16)
                upd = lax.dot_general(v_c.astype(BF16), kl_c, TN, preferred_element_type=F32)
                st_ref[hd] = st * jnp.exp(bl) + upd
                return 0

            lax.fori_loop(0, nc, chunk_body, 0)

    o = o_scr[...]
    gate = g_scr[...]
    gout = gout_ref[...]
    for hd in range(HG_HEADS):
        lo = hd * HG_DV
        o_ref[:, lo:lo + HG_DV] = (_rms(o[:, lo:lo + HG_DV], gout) * gate[:, lo:lo + HG_DV]).astype(BF16)


def _hgrn(x2, gmix, whg, loglb, log1mlb, gout, tri, cmask, batch, seq, tm):
    nt = seq // tm
    return pl.pallas_call(
        functools.partial(_hgrn_kernel, tm=tm),
        grid=(batch, nt),
        in_specs=[
            pl.BlockSpec((tm, D_MODEL), lambda b, i: (b * nt + i, 0)),
            _const_spec(gmix.shape), _const_spec(whg.shape), _const_spec(loglb.shape),
            _const_spec(log1mlb.shape), _const_spec(gout.shape), _const_spec(tri.shape),
            _const_spec(cmask.shape),
        ],
        out_specs=pl.BlockSpec((tm, HG_WIDTH), lambda b, i: (b * nt + i, 0)),
        out_shape=jax.ShapeDtypeStruct((batch * seq, HG_WIDTH), BF16),
        scratch_shapes=[
            pltpu.VMEM((HG_HEADS, HG_DV, HG_DK), F32),
            pltpu.VMEM((HG_HEADS, HG_DV, HG_DK), F32),
            pltpu.VMEM((tm, HG_KW), F32),
            pltpu.VMEM((tm, HG_KW), F32),
            pltpu.VMEM((tm, HG_WIDTH), F32),
            pltpu.VMEM((tm, HG_KW), F32),
            pltpu.VMEM((tm, HG_WIDTH), F32),
            pltpu.VMEM((tm, HG_WIDTH), F32),
        ],
        compiler_params=_params(("parallel", "arbitrary")),
        name="hgrn",
    )(x2, gmix, whg, loglb, log1mlb, gout, tri, cmask)


def _merge_kernel(x_ref, ya_ref, yb_ref, gmix_ref, wg_ref, wpa_ref, wpb_ref, wout_ref, o_ref):
    x = x_ref[...]
    h = _rms(x, gmix_ref[...]).astype(BF16)
    ya, yb = ya_ref[...], yb_ref[...]
    acc = x
    w = D_MODEL // MERGE_SPLIT
    for j in range(MERGE_SPLIT):
        cols = slice(j * w, (j + 1) * w)
        g_a = _dot(h, wg_ref[:, cols])
        g_b = _dot(h, wg_ref[:, D_MODEL + j * w:D_MODEL + (j + 1) * w])
        y = (jax.nn.sigmoid(g_a) * _dot(ya, wpa_ref[:, cols])
             + jax.nn.sigmoid(g_b) * _dot(yb, wpb_ref[:, cols]))
        acc = acc + _dot(y.astype(BF16), wout_ref[cols, :])
    o_ref[...] = acc


def _merge(x2, ya, yb, gmix, wg, wpa, wpb, wout, tm):
    n = x2.shape[0]
    return pl.pallas_call(
        _merge_kernel,
        grid=(n // tm,),
        in_specs=[
            pl.BlockSpec((tm, D_MODEL), lambda i: (i, 0)),
            pl.BlockSpec((tm, ya.shape[1]), lambda i: (i, 0)),
            pl.BlockSpec((tm, yb.shape[1]), lambda i: (i, 0)),
            _const_spec(gmix.shape), _const_spec(wg.shape), _const_spec(wpa.shape),
            _const_spec(wpb.shape), _const_spec(wout.shape),
        ],
        out_specs=pl.BlockSpec((tm, D_MODEL), lambda i: (i, 0)),
        out_shape=jax.ShapeDtypeStruct((n, D_MODEL), F32),
        compiler_params=_params(("parallel",)),
        name="merge",
    )(x2, ya, yb, gmix, wg, wpa, wpb, wout)


def _ffn_kernel(x_ref, gffn_ref, wgate_ref, wup_ref, wdown_ref, o_ref, *, n_split):
    x = x_ref[...]
    h = _rms(x, gffn_ref[...]).astype(BF16)
    d_ff = wgate_ref.shape[1]
    w = d_ff // n_split
    acc = x
    for j in range(n_split):
        gate = _dot(h, wgate_ref[:, j * w:(j + 1) * w])
        up = _dot(h, wup_ref[:, j * w:(j + 1) * w])
        act = (gate * jax.nn.sigmoid(gate) * up).astype(BF16)
        acc = acc + _dot(act, wdown_ref[j * w:(j + 1) * w, :])
    o_ref[...] = acc


def _ffn(x2, gffn, wgate, wup, wdown, tm, n_split):
    n = x2.shape[0]
    return pl.pallas_call(
        functools.partial(_ffn_kernel, n_split=n_split),
        grid=(n // tm,),
        in_specs=[
            pl.BlockSpec((tm, D_MODEL), lambda i: (i, 0)),
            _const_spec(gffn.shape), _const_spec(wgate.shape), _const_spec(wup.shape),
            _const_spec(wdown.shape),
        ],
        out_specs=pl.BlockSpec((tm, D_MODEL), lambda i: (i, 0)),
        out_shape=jax.ShapeDtypeStruct((n, D_MODEL), F32),
        compiler_params=_params(("parallel",)),
        name="ffn",
    )(x2, gffn, wgate, wup, wdown)


def _rope_tables(seq, gain, scale):
    pos = jnp.arange(seq, dtype=F32)
    inv_freq = 1.0 / (ROPE_THETA ** (jnp.arange(0, MLA_ROPE, 2, dtype=F32) / MLA_ROPE))
    ang = pos[:, None] * inv_freq[None, :]
    cos, sin = jnp.cos(ang), jnp.sin(ang)
    half = MLA_ROPE // 2
    g_nope, g_rope = gain[:MLA_NOPE], gain[MLA_NOPE:]
    nope = jnp.broadcast_to(g_nope[None, :], (seq, MLA_NOPE))
    c_rope = jnp.concatenate([cos, cos], axis=1) * g_rope[None, :]
    g_partner = jnp.concatenate([g_rope[half:], g_rope[:half]])
    s_rope = jnp.concatenate([-sin, sin], axis=1) * g_partner[None, :]
    zeros = jnp.zeros((seq, MLA_NOPE), F32)
    q_tab = jnp.concatenate([nope, c_rope, s_rope], axis=1) * scale
    ctab = jnp.concatenate([nope, c_rope, c_rope], axis=1) * scale
    stab = jnp.concatenate([zeros, s_rope, s_rope], axis=1) * scale
    return q_tab, ctab, stab


def _layer_params(l, seq, tm_hg, w_in, mla_w_uq, mla_w_ukv, mla_q_norm, mla_k_norm, lower_bounds):
    o = [0]
    for s in IN_SPLITS:
        o.append(o[-1] + s)
    wi = w_in[l]
    half = MLA_ROPE // 2
    z64 = jnp.zeros((D_MODEL, MLA_NOPE), F32)
    w_kpe = wi[:, o[2]:o[3]]
    w_kpe_sw = jnp.concatenate([w_kpe[:, half:], w_kpe[:, :half]], axis=1)
    wmla = jnp.concatenate([wi[:, o[0]:o[2]], z64, w_kpe, w_kpe, z64, w_kpe_sw, w_kpe_sw], axis=1).astype(BF16)
    whg = wi[:, o[3]:o[7]].astype(BF16)
    wgates = wi[:, o[7]:o[9]].astype(BF16)

    uq = mla_w_uq[l].reshape(Q_LORA, MLA_HEADS, MLA_QK)
    uq_nope, uq_rope = uq[..., :MLA_NOPE], uq[..., MLA_NOPE:]
    uq_rope_sw = jnp.concatenate([uq_rope[..., half:], uq_rope[..., :half]], axis=-1)
    wuq = jnp.concatenate([uq_nope, uq_rope, uq_rope_sw], axis=-1).reshape(
        Q_LORA, MLA_HEADS * HEAD_PAD).astype(BF16)

    ukv = mla_w_ukv[l].reshape(KV_LORA, MLA_HEADS, MLA_NOPE + MLA_V)
    zk64 = jnp.zeros((KV_LORA, MLA_HEADS, HEAD_PAD - MLA_NOPE), F32)
    wuk = jnp.concatenate([ukv[..., :MLA_NOPE], zk64], axis=-1).reshape(KV_LORA, MLA_HEADS * HEAD_PAD)
    wuv = ukv[..., MLA_NOPE:].reshape(KV_LORA, MLA_HEADS * MLA_V)
    wukv = jnp.concatenate([wuk, wuv], axis=1).astype(BF16)

    tq_tab, _, _ = _rope_tables(seq, mla_q_norm[l], MLA_QK ** -0.5 * LOG2_E)
    _, ck, sk = _rope_tables(seq, mla_k_norm[l], 1.0)
    norm_mask = (jnp.arange(HEAD_PAD) < MLA_QK).astype(F32)[None, :]

    lb = lower_bounds[l][None, :]
    r = jnp.arange(tm_hg)
    tri = ((r[:, None] // HG_CHUNK == r[None, :] // HG_CHUNK) & (r[:, None] >= r[None, :])).astype(BF16)
    nc = tm_hg // HG_CHUNK
    cmask = (jnp.arange(nc * HG_DV)[:, None] // HG_DV == r[None, :] // HG_CHUNK).astype(BF16)
    return dict(wmla=wmla, whg=whg, wgates=wgates, wuq=wuq, wukv=wukv, tabs=(tq_tab, ck, sk), norm_mask=norm_mask,
                loglb=jnp.log(lb), log1mlb=jnp.log1p(-lb), tri=tri, cmask=cmask)


def kernel(x, norm_mix, w_in, mla_norm_cq, mla_w_uq, mla_norm_ckv, mla_w_ukv, mla_q_norm, mla_k_norm,
           hg_lb_logits, hg_out_norm, w_proj_a, w_proj_b, w_out, norm_ffn, w_gate, w_up, w_down):
    batch, seq, d = x.shape
    assert d == D_MODEL
    depth = w_in.shape[0]
    tm = min(256, seq)
    tm_prep = min(2 * tm, seq)
    tm_hg = min(2 * HG_SUB, seq)
    tm_ffn = min(1024, seq)
    assert all(seq % t == 0 for t in (tm, tm_prep, tm_hg, tm_ffn)) and tm % HG_CHUNK == 0

    p = jax.nn.softmax(hg_lb_logits.astype(F32), axis=0)
    lower_bounds = jnp.maximum(jnp.cumsum(p, axis=0) - p[0:1], 0.0)

    x2 = x.reshape(batch * seq, d)
    for l in range(depth):
        lp = _layer_params(l, seq, min(HG_SUB, seq), w_in, mla_w_uq, mla_w_ukv, mla_q_norm, mla_k_norm, lower_bounds)
        gmix = norm_mix[l][None, :]
        qt, k, vt = _mla_prep(x2, gmix, lp["wmla"], mla_norm_cq[l][None, :], lp["wuq"],
                             mla_norm_ckv[l][None, :], lp["wukv"], lp["norm_mask"], lp["tabs"], seq, tm_prep, tm)
        y_a = _mla_attn(qt, k, vt, batch, seq, tm)
        y_b = _hgrn(x2, gmix, lp["whg"], lp["loglb"], lp["log1mlb"], hg_out_norm[l][None, :],
                    lp["tri"], lp["cmask"], batch, seq, tm_hg)
        x2 = _merge(x2, y_a, y_b, gmix, lp["wgates"], w_proj_a[l].astype(BF16),
                    w_proj_b[l].astype(BF16), w_out[l].astype(BF16), tm_ffn)
        x2 = _ffn(x2, norm_ffn[l][None, :], w_gate[l].astype(BF16), w_up[l].astype(BF16),
                  w_down[l].astype(BF16), tm_ffn, FFN_SPLIT)
    return x2.reshape(batch, seq, d)
```

```python
import functools

import jax
import jax.numpy as jnp
from jax import lax
from jax.experimental import pallas as pl
from jax.experimental.pallas import tpu as pltpu

D_MODEL = 1024
MLA_HEADS = 8
MLA_NOPE = 64
MLA_ROPE = 32
MLA_QK = MLA_NOPE + MLA_ROPE
MLA_V = 64
Q_LORA = 384
KV_LORA = 256
ROPE_THETA = 10000.0
HG_HEADS = 4
HG_DK = 128
HG_DV = 128
HG_KW = HG_HEADS * HG_DK
HG_WIDTH = HG_HEADS * HG_DV
NORM_EPS = 1e-6
IN_SPLITS = (Q_LORA, KV_LORA, MLA_ROPE, HG_KW, HG_KW, HG_WIDTH, HG_WIDTH, D_MODEL, D_MODEL)

LANES = 128
HEAD_PAD = LANES
ATTN_GROUP = 4
HG_CHUNK = 32
HG_SUB = 256
HG_SAFE_DECAY = 60.0
LOG2_E = 1.4426950408889634
MLA_IN_W = 896
MERGE_SPLIT = 2
FFN_SPLIT = 11
VMEM_LIMIT = 56 * 1024 * 1024

BF16 = jnp.bfloat16
F32 = jnp.float32
NT = (((1,), (1,)), ((), ()))
TN = (((0,), (0,)), ((), ()))


def _rms(x, g):
    return x * lax.rsqrt(jnp.mean(x * x, axis=-1, keepdims=True) + NORM_EPS) * g


def _dot(a, b):
    return jnp.dot(a, b, preferred_element_type=F32)


def _const_spec(shape):
    nd = len(shape)
    return pl.BlockSpec(shape, lambda *_: (0,) * nd, pipeline_mode=pl.Buffered(1))


def _params(sem):
    return pltpu.CompilerParams(dimension_semantics=sem, vmem_limit_bytes=VMEM_LIMIT)


def _attn_kernel(qt_ref, k_ref, vt_ref, o_ref, s_scr, m_scr, *, tq):
    i = pl.program_id(2)
    nq = pl.num_programs(2)
    kv_idx = lax.broadcasted_iota(jnp.int32, (tq, tq), 0)
    q_idx = lax.broadcasted_iota(jnp.int32, (tq, tq), 1)
    causal = q_idx >= kv_idx

    def scores(j, iq):
        start = pl.multiple_of(j * tq, tq)
        return tuple(_dot(k_ref[pl.ds(start, tq), hh * HEAD_PAD:(hh + 1) * HEAD_PAD],
                          qt_ref[iq, hh * HEAD_PAD:(hh + 1) * HEAD_PAD, :]) for hh in range(ATTN_GROUP))

    def stash(slot, s_all):
        for hh in range(ATTN_GROUP):
            s_scr[slot, hh] = s_all[hh]
            m_scr[slot, hh] = jnp.max(s_all[hh], axis=0, keepdims=True)

    def update(j, slot, carry, masked):
        vt = vt_ref[j]
        out = []
        for hh in range(ATTN_GROUP):
            m, l, acc = carry[hh]
            s = s_scr[slot, hh]
            if masked:
                s = jnp.where(causal, s, -jnp.inf)
                s_max = jnp.max(s, axis=0, keepdims=True)
            else:
                s_max = m_scr[slot, hh]
            m_new = jnp.maximum(m, s_max)
            alpha = jnp.exp2(m - m_new)
            p = jnp.exp2(s - m_new)
            l = alpha * l + jnp.sum(p, axis=0, keepdims=True)
            acc = alpha * acc + _dot(vt[hh * MLA_V:(hh + 1) * MLA_V, :], p.astype(BF16))
            out.append((m_new, l, acc))
        return tuple(out)

    @pl.when(i == 0)
    def _():
        stash(0, scores(0, 0))

    init = tuple((jnp.full((1, tq), -jnp.inf, F32), jnp.zeros((1, tq), F32), jnp.zeros((MLA_V, tq), F32))
                 for _ in range(ATTN_GROUP))

    def body(j, carry):
        s_next = scores(j + 1, i)
        carry = update(j, lax.rem(j, 2), carry, False)
        stash(lax.rem(j + 1, 2), s_next)
        return carry

    carry = lax.fori_loop(0, i, body, init)
    s_ahead = scores(0, jnp.minimum(i + 1, nq - 1))
    carry = update(i, lax.rem(i, 2), carry, True)
    stash(0, s_ahead)
    o_t = jnp.concatenate([acc / l for _, l, acc in carry], axis=0)
    o_ref[...] = o_t.T.astype(BF16)


def _mla_attn(qt, k, vt, batch, seq, tq):
    nq = seq // tq
    return pl.pallas_call(
        functools.partial(_attn_kernel, tq=tq),
        grid=(batch, MLA_HEADS // ATTN_GROUP, nq),
        in_specs=[
            pl.BlockSpec((None, nq, ATTN_GROUP * HEAD_PAD, tq), lambda b, g, i: (b, 0, g, 0)),
            pl.BlockSpec((seq, ATTN_GROUP * HEAD_PAD), lambda b, g, i: (b, g)),
            pl.BlockSpec((None, None, nq, ATTN_GROUP * MLA_V, tq), lambda b, g, i: (b, g, 0, 0, 0)),
        ],
        out_specs=pl.BlockSpec((tq, ATTN_GROUP * MLA_V), lambda b, g, i: (b * nq + i, g)),
        out_shape=jax.ShapeDtypeStruct((batch * seq, MLA_HEADS * MLA_V), BF16),
        scratch_shapes=[
            pltpu.VMEM((2, ATTN_GROUP, tq, tq), F32),
            pltpu.VMEM((2, ATTN_GROUP, 1, tq), F32),
        ],
        compiler_params=_params(("parallel", "parallel", "arbitrary")),
        name="mla_attn",
    )(qt, k, vt)


def _split3(x):
    a = x.astype(BF16)
    r = x - a.astype(F32)
    b = r.astype(BF16)
    c = (r - b.astype(F32)).astype(BF16)
    return a, b, c


def _mixer_in_kernel(x_ref, gmix_ref, wcat_ref, gcq_ref, wuq_ref, gckv_ref, wukv_ref, nmask_ref,
                     tq_ref, ck_ref, sk_ref, loglb_ref, log1mlb_ref, gout_ref, tri_ref, cmask_ref,
                     qt_ref, k_ref, vt_ref, o_ref,
                     st_ref, stbak_ref, q_scr, k_scr, v_scr, b_scr, o_scr, g_scr, *, tm):
    S = HG_SUB
    C = HG_CHUNK
    n_sub = tm // S
    ncs = S // C
    nc = tm // C

    @pl.when(pl.program_id(1) == 0)
    def _():
        st_ref[...] = jnp.zeros_like(st_ref)

    stbak_ref[...] = st_ref[...]
    gmix = gmix_ref[...]
    zzs = [_dot(_rms(x_ref[u * S:(u + 1) * S, :], gmix).astype(BF16), wcat_ref[...]) for u in range(n_sub)]
    zs = [zz[:, MLA_IN_W:] for zz in zzs]
    nmask = nmask_ref[...]
    inv_qk = 1.0 / MLA_QK
    gw = ATTN_GROUP * MLA_V

    tri = tri_ref[...]
    cmask = cmask_ref[...]
    row = lax.broadcasted_iota(jnp.int32, (S, S), 0)
    col = lax.broadcasted_iota(jnp.int32, (S, S), 1)
    causal = (row >= col) & (row // C == col // C)
    a = loglb_ref[...]
    log1mlb = log1mlb_ref[...]
    sts = [st_ref[hd] for hd in range(HG_HEADS)]
    b_mins = []
    for u in range(n_sub):
        rows = slice(u * S, (u + 1) * S)
        zc = zzs[u][:, :MLA_IN_W]
        c_q = zc[:, :Q_LORA]
        c_kv = zc[:, Q_LORA:Q_LORA + KV_LORA]
        kpe_a = zc[:, 640:768]
        kpe_b = zc[:, 768:896]
        q2 = _dot(_rms(c_q, gcq_ref[...]).astype(BF16), wuq_ref[...])
        kv2 = _dot(_rms(c_kv, gckv_ref[...]).astype(BF16), wukv_ref[...])
        tq_tab, ck, sk = tq_ref[rows, :], ck_ref[rows, :], sk_ref[rows, :]
        for hd in range(MLA_HEADS):
            lo = hd * HEAD_PAD
            qa = q2[:, lo:lo + HEAD_PAD]
            rq = lax.rsqrt(jnp.sum(qa * qa * nmask, axis=-1, keepdims=True) * inv_qk + NORM_EPS)
            qt_ref[u, lo:lo + HEAD_PAD, :] = (qa * tq_tab * rq).astype(BF16).T
            ka = kv2[:, lo:lo + HEAD_PAD] + kpe_a
            rk = lax.rsqrt(jnp.sum(ka * ka * nmask, axis=-1, keepdims=True) * inv_qk + NORM_EPS)
            k_ref[rows, lo:lo + HEAD_PAD] = ((ka * ck + kpe_b * sk) * rk).astype(BF16)
        v = kv2[:, MLA_HEADS * HEAD_PAD:]
        for g in range(MLA_HEADS // ATTN_GROUP):
            vt_ref[g, u] = v[:, g * gw:(g + 1) * gw].astype(BF16).T

        z = zs[u]
        zq = z[:, :HG_KW]
        zf = z[:, HG_KW:2 * HG_KW]
        zi = z[:, 2 * HG_KW:2 * HG_KW + HG_WIDTH]
        zg = z[:, 2 * HG_KW + HG_WIDTH:]
        log_sig = jnp.minimum(zf, 0.0) - jnp.log(1.0 + jnp.exp(-jnp.abs(zf)))
        c = log1mlb + log_sig
        log_f = jnp.maximum(a, c) + jnp.log(1.0 + jnp.exp(-jnp.abs(a - c)))
        kk = 1.0 - jnp.exp(log_f)
        l1, l2, l3 = _split3(log_f)
        b = _dot(tri, l1) + _dot(tri, l2) + _dot(tri, l3)
        b3 = b.reshape(ncs, C, HG_KW)
        b_last = b3[:, C - 1:C, :]
        q_scr[rows, :] = zq
        k_scr[rows, :] = kk
        v_scr[rows, :] = zi
        b_scr[rows, :] = b
        g_scr[rows, :] = zg * jax.nn.sigmoid(zg)
        b_mins.append(jnp.min(b))

        qe = (zq * jnp.exp(b)).astype(BF16)
        ke = (kk * jnp.exp(jnp.minimum(-b, HG_SAFE_DECAY + 20.0))).astype(BF16)
        kl = (kk.reshape(ncs, C, HG_KW) * jnp.exp(b_last - b3)).reshape(S, HG_KW).astype(BF16)
        e_last = jnp.exp(b_last)
        vb = zi.astype(BF16)
        vt = vb.T
        for hd in range(HG_HEADS):
            lo = hd * HG_DK
            qe_h = qe[:, lo:lo + HG_DK]
            att = lax.dot_general(qe_h, ke[:, lo:lo + HG_DK], NT, preferred_element_type=F32)
            att = jnp.where(causal, att, 0.0).astype(BF16)
            o_intra = _dot(att, vb[:, lo:lo + HG_DV])
            vt_h = vt[lo:lo + HG_DV, :]
            upd = _dot(jnp.concatenate([vt_h] * ncs, axis=0) * cmask, kl[:, lo:lo + HG_DK])
            st = sts[hd]
            for ci in range(ncs):
                r0 = ci * C
                o_inter = lax.dot_general(qe_h[r0:r0 + C], st.astype(BF16), NT, preferred_element_type=F32)
                o_scr[u * S + r0:u * S + r0 + C, lo:lo + HG_DV] = o_intra[r0:r0 + C] + o_inter
                st = st * e_last[ci, :, lo:lo + HG_DK] + upd[ci * HG_DV:(ci + 1) * HG_DV]
            sts[hd] = st
    for hd in range(HG_HEADS):
        st_ref[hd] = sts[hd]
    b_min = b_mins[0]
    for u in range(1, n_sub):
        b_min = jnp.minimum(b_min, b_mins[u])
    unsafe = b_min < -HG_SAFE_DECAY

    @pl.when(unsafe)
    def _():
        st_ref[...] = stbak_ref[...]
        t_idx = lax.broadcasted_iota(jnp.int32, (C, 1), 0)
        for hd in range(HG_HEADS):
            lo = hd * HG_DK

            def chunk_body(ci, _, lo=lo, hd=hd):
                r0 = pl.multiple_of(ci * C, C)
                q_c = q_scr[pl.ds(r0, C), lo:lo + HG_DK]
                k_c = k_scr[pl.ds(r0, C), lo:lo + HG_DK]
                v_c = v_scr[pl.ds(r0, C), lo:lo + HG_DV]
                b_c = b_scr[pl.ds(r0, C), lo:lo + HG_DK]
                bl = b_c[C - 1:C, :]

                def s_body(s, o_acc):
                    pick = t_idx == s
                    ks = jnp.sum(jnp.where(pick, k_c, 0.0), axis=0, keepdims=True)
                    bs = jnp.sum(jnp.where(pick, b_c, 0.0), axis=0, keepdims=True)
                    vs = jnp.sum(jnp.where(pick, v_c, 0.0), axis=0, keepdims=True)
                    dec = jnp.exp(jnp.minimum(b_c - bs, 0.0))
                    w = jnp.sum(q_c * ks * dec, axis=-1, keepdims=True)
                    w = jnp.where(t_idx >= s, w, 0.0)
                    return o_acc + w * vs

                o_intra = lax.fori_loop(0, C, s_body, jnp.zeros((C, HG_DV), F32))
                st = st_ref[hd]
                qe_c = (q_c * jnp.exp(b_c)).astype(BF16)
                o_inter = lax.dot_general(qe_c, st.astype(BF16), NT, preferred_element_type=F32)
                o_scr[pl.ds(r0, C), lo:lo + HG_DV] = o_intra + o_inter
                kl_c = (k_c * jnp.exp(bl - b_c)).astype(BF16)
                upd = lax.dot_general(v_c.astype(BF16), kl_c, TN, preferred_element_type=F32)
                st_ref[hd] = st * jnp.exp(bl) + upd
                return 0

            lax.fori_loop(0, nc, chunk_body, 0)

    o = o_scr[...]
    gate = g_scr[...]
    gout = gout_ref[...]
    for hd in range(HG_HEADS):
        lo = hd * HG_DV
        o_ref[:, lo:lo + HG_DV] = (_rms(o[:, lo:lo + HG_DV], gout) * gate[:, lo:lo + HG_DV]).astype(BF16)


def _mixer_in(x2, gmix, wcat, gcq, wuq, gckv, wukv, nmask, tabs, loglb, log1mlb, gout, tri, cmask,
              batch, seq, tm):
    nt = seq // tm
    n_sub = tm // HG_SUB
    n = batch * seq
    tab_spec = pl.BlockSpec((tm, LANES), lambda b, i: (i, 0))

    def row_spec(width):
        return pl.BlockSpec((tm, width), lambda b, i: (b * nt + i, 0))

    return pl.pallas_call(
        functools.partial(_mixer_in_kernel, tm=tm),
        grid=(batch, nt),
        in_specs=[
            row_spec(D_MODEL),
            _const_spec(gmix.shape), _const_spec(wcat.shape), _const_spec(gcq.shape),
            _const_spec(wuq.shape), _const_spec(gckv.shape), _const_spec(wukv.shape),
            _const_spec(nmask.shape), tab_spec, tab_spec, tab_spec,
            _const_spec(loglb.shape), _const_spec(log1mlb.shape), _const_spec(gout.shape),
            _const_spec(tri.shape), _const_spec(cmask.shape),
        ],
        out_specs=[
            pl.BlockSpec((None, n_sub, MLA_HEADS * HEAD_PAD, HG_SUB), lambda b, i: (b, i, 0, 0)),
            row_spec(MLA_HEADS * HEAD_PAD),
            pl.BlockSpec((None, MLA_HEADS // ATTN_GROUP, n_sub, ATTN_GROUP * MLA_V, HG_SUB),
                         lambda b, i: (b, 0, i, 0, 0)),
            row_spec(HG_WIDTH),
        ],
        out_shape=[
            jax.ShapeDtypeStruct((batch, seq // HG_SUB, MLA_HEADS * HEAD_PAD, HG_SUB), BF16),
            jax.ShapeDtypeStruct((n, MLA_HEADS * HEAD_PAD), BF16),
            jax.ShapeDtypeStruct((batch, MLA_HEADS // ATTN_GROUP, seq // HG_SUB, ATTN_GROUP * MLA_V, HG_SUB), BF16),
            jax.ShapeDtypeStruct((n, HG_WIDTH), BF16),
        ],
        scratch_shapes=[
            pltpu.VMEM((HG_HEADS, HG_DV, HG_DK), F32),
            pltpu.VMEM((HG_HEADS, HG_DV, HG_DK), F32),
            pltpu.VMEM((tm, HG_KW), F32),
            pltpu.VMEM((tm, HG_KW), F32),
            pltpu.VMEM((tm, HG_WIDTH), F32),
            pltpu.VMEM((tm, HG_KW), F32),
            pltpu.VMEM((tm, HG_WIDTH), F32),
            pltpu.VMEM((tm, HG_WIDTH), F32),
        ],
        compiler_params=_params(("parallel", "arbitrary")),
        name="mixer_in",
    )(x2, gmix, wcat, gcq, wuq, gckv, wukv, nmask, *tabs, loglb, log1mlb, gout, tri, cmask)


def _merge_kernel(x_ref, ya_ref, yb_ref, gmix_ref, wg_ref, wpa_ref, wpb_ref, wout_ref, o_ref):
    x = x_ref[...]
    h = _rms(x, gmix_ref[...]).astype(BF16)
    ya, yb = ya_ref[...], yb_ref[...]
    acc = x
    w = D_MODEL // MERGE_SPLIT
    for j in range(MERGE_SPLIT):
        cols = slice(j * w, (j + 1) * w)
        g_a = _dot(h, wg_ref[:, cols])
        g_b = _dot(h, wg_ref[:, D_MODEL + j * w:D_MODEL + (j + 1) * w])
        y = (jax.nn.sigmoid(g_a) * _dot(ya, wpa_ref[:, cols])
             + jax.nn.sigmoid(g_b) * _dot(yb, wpb_ref[:, cols]))
        acc = acc + _dot(y.astype(BF16), wout_ref[cols, :])
    o_ref[...] = acc


def _merge(x2, ya, yb, gmix, wg, wpa, wpb, wout, tm):
    n = x2.shape[0]
    return pl.pallas_call(
        _merge_kernel,
        grid=(n // tm,),
        in_specs=[
            pl.BlockSpec((tm, D_MODEL), lambda i: (i, 0)),
            pl.BlockSpec((tm, ya.shape[1]), lambda i: (i, 0)),
            pl.BlockSpec((tm, yb.shape[1]), lambda i: (i, 0)),
            _const_spec(gmix.shape), _const_spec(wg.shape), _const_spec(wpa.shape),
            _const_spec(wpb.shape), _const_spec(wout.shape),
        ],
        out_specs=pl.BlockSpec((tm, D_MODEL), lambda i: (i, 0)),
        out_shape=jax.ShapeDtypeStruct((n, D_MODEL), F32),
        compiler_params=_params(("parallel",)),
        name="merge",
    )(x2, ya, yb, gmix, wg, wpa, wpb, wout)


def _ffn_kernel(x_ref, gffn_ref, wgate_ref, wup_ref, wdown_ref, o_ref, *, n_split):
    x = x_ref[...]
    h = _rms(x, gffn_ref[...]).astype(BF16)
    d_ff = wgate_ref.shape[1]
    w = d_ff // n_split
    acc = x
    for j in range(n_split):
        gate = _dot(h, wgate_ref[:, j * w:(j + 1) * w])
        up = _dot(h, wup_ref[:, j * w:(j + 1) * w])
        act = (gate * jax.nn.sigmoid(gate) * up).astype(BF16)
        acc = acc + _dot(act, wdown_ref[j * w:(j + 1) * w, :])
    o_ref[...] = acc


def _ffn(x2, gffn, wgate, wup, wdown, tm, n_split):
    n = x2.shape[0]
    return pl.pallas_call(
        functools.partial(_ffn_kernel, n_split=n_split),
        grid=(n // tm,),
        in_specs=[
            pl.BlockSpec((tm, D_MODEL), lambda i: (i, 0)),
            _const_spec(gffn.shape), _const_spec(wgate.shape), _const_spec(wup.shape),
            _const_spec(wdown.shape),
        ],
        out_specs=pl.BlockSpec((tm, D_MODEL), lambda i: (i, 0)),
        out_shape=jax.ShapeDtypeStruct((n, D_MODEL), F32),
        compiler_params=_params(("parallel",)),
        name="ffn",
    )(x2, gffn, wgate, wup, wdown)


def _rope_tables(seq, gain, scale):
    pos = jnp.arange(seq, dtype=F32)
    inv_freq = 1.0 / (ROPE_THETA ** (jnp.arange(0, MLA_ROPE, 2, dtype=F32) / MLA_ROPE))
    ang = pos[:, None] * inv_freq[None, :]
    cos, sin = jnp.cos(ang), jnp.sin(ang)
    half = MLA_ROPE // 2
    g_nope, g_rope = gain[:MLA_NOPE], gain[MLA_NOPE:]
    nope = jnp.broadcast_to(g_nope[None, :], (seq, MLA_NOPE))
    c_rope = jnp.concatenate([cos, cos], axis=1) * g_rope[None, :]
    g_partner = jnp.concatenate([g_rope[half:], g_rope[:half]])
    s_rope = jnp.concatenate([-sin, sin], axis=1) * g_partner[None, :]
    zeros = jnp.zeros((seq, MLA_NOPE), F32)
    q_tab = jnp.concatenate([nope, c_rope, s_rope], axis=1) * scale
    ctab = jnp.concatenate([nope, c_rope, c_rope], axis=1) * scale
    stab = jnp.concatenate([zeros, s_rope, s_rope], axis=1) * scale
    return q_tab, ctab, stab


def _layer_params(l, seq, tm_hg, w_in, mla_w_uq, mla_w_ukv, mla_q_norm, mla_k_norm, lower_bounds):
    o = [0]
    for s in IN_SPLITS:
        o.append(o[-1] + s)
    wi = w_in[l]
    half = MLA_ROPE // 2
    z64 = jnp.zeros((D_MODEL, MLA_NOPE), F32)
    w_kpe = wi[:, o[2]:o[3]]
    w_kpe_sw = jnp.concatenate([w_kpe[:, half:], w_kpe[:, :half]], axis=1)
    wmla = jnp.concatenate([wi[:, o[0]:o[2]], z64, w_kpe, w_kpe, z64, w_kpe_sw, w_kpe_sw], axis=1).astype(BF16)
    whg = wi[:, o[3]:o[7]].astype(BF16)
    wgates = wi[:, o[7]:o[9]].astype(BF16)

    uq = mla_w_uq[l].reshape(Q_LORA, MLA_HEADS, MLA_QK)
    uq_nope, uq_rope = uq[..., :MLA_NOPE], uq[..., MLA_NOPE:]
    uq_rope_sw = jnp.concatenate([uq_rope[..., half:], uq_rope[..., :half]], axis=-1)
    wuq = jnp.concatenate([uq_nope, uq_rope, uq_rope_sw], axis=-1).reshape(
        Q_LORA, MLA_HEADS * HEAD_PAD).astype(BF16)

    ukv = mla_w_ukv[l].reshape(KV_LORA, MLA_HEADS, MLA_NOPE + MLA_V)
    zk64 = jnp.zeros((KV_LORA, MLA_HEADS, HEAD_PAD - MLA_NOPE), F32)
    wuk = jnp.concatenate([ukv[..., :MLA_NOPE], zk64], axis=-1).reshape(KV_LORA, MLA_HEADS * HEAD_PAD)
    wuv = ukv[..., MLA_NOPE:].reshape(KV_LORA, MLA_HEADS * MLA_V)
    wukv = jnp.concatenate([wuk, wuv], axis=1).astype(BF16)

    tq_tab, _, _ = _rope_tables(seq, mla_q_norm[l], MLA_QK ** -0.5 * LOG2_E)
    _, ck, sk = _rope_tables(seq, mla_k_norm[l], 1.0)
    norm_mask = (jnp.arange(HEAD_PAD) < MLA_QK).astype(F32)[None, :]

    lb = lower_bounds[l][None, :]
    r = jnp.arange(tm_hg)
    tri = ((r[:, None] // HG_CHUNK == r[None, :] // HG_CHUNK) & (r[:, None] >= r[None, :])).astype(BF16)
    nc = tm_hg // HG_CHUNK
    cmask = (jnp.arange(nc * HG_DV)[:, None] // HG_DV == r[None, :] // HG_CHUNK).astype(BF16)
    return dict(wcat=jnp.concatenate([wmla, whg], axis=1), wgates=wgates, wuq=wuq, wukv=wukv,
                tabs=(tq_tab, ck, sk), norm_mask=norm_mask,
                loglb=jnp.log(lb), log1mlb=jnp.log1p(-lb), tri=tri, cmask=cmask)


def kernel(x, norm_mix, w_in, mla_norm_cq, mla_w_uq, mla_norm_ckv, mla_w_ukv, mla_q_norm, mla_k_norm,
           hg_lb_logits, hg_out_norm, w_proj_a, w_proj_b, w_out, norm_ffn, w_gate, w_up, w_down):
    batch, seq, d = x.shape
    assert d == D_MODEL
    depth = w_in.shape[0]
    tm = min(HG_SUB, seq)
    tm_hg = min(2 * HG_SUB, seq)
    tm_ffn = min(1024, seq)
    assert all(seq % t == 0 for t in (tm, tm_hg, tm_ffn))

    p = jax.nn.softmax(hg_lb_logits.astype(F32), axis=0)
    lower_bounds = jnp.maximum(jnp.cumsum(p, axis=0) - p[0:1], 0.0)

    x2 = x.reshape(batch * seq, d)
    for l in range(depth):
        lp = _layer_params(l, seq, min(HG_SUB, seq), w_in, mla_w_uq, mla_w_ukv, mla_q_norm, mla_k_norm, lower_bounds)
        gmix = norm_mix[l][None, :]
        qt, k, vt, y_b = _mixer_in(x2, gmix, lp["wcat"], mla_norm_cq[l][None, :], lp["wuq"],
                                   mla_norm_ckv[l][None, :], lp["wukv"], lp["norm_mask"], lp["tabs"],
                                   lp["loglb"], lp["log1mlb"], hg_out_norm[l][None, :], lp["tri"], lp["cmask"],
                                   batch, seq, tm_hg)
        y_a = _mla_attn(qt, k, vt, batch, seq, tm)
        x2 = _merge(x2, y_a, y_b, gmix, lp["wgates"], w_proj_a[l].astype(BF16),
                    w_proj_b[l].astype(BF16), w_out[l].astype(BF16), tm_ffn)
        x2 = _ffn(x2, norm_ffn[l][None, :], w_gate[l].astype(BF16), w_up[l].astype(BF16),
                  w_down[l].astype(BF16), tm_ffn, FFN_SPLIT)
    return x2.reshape(batch, seq, d)
```

```python
import functools

import jax
import jax.numpy as jnp
from jax import lax
from jax.experimental import pallas as pl
from jax.experimental.pallas import tpu as pltpu

D_MODEL = 1024
MLA_HEADS = 8
MLA_NOPE = 64
MLA_ROPE = 32
MLA_QK = MLA_NOPE + MLA_ROPE
MLA_V = 64
Q_LORA = 384
KV_LORA = 256
ROPE_THETA = 10000.0
HG_HEADS = 4
HG_DK = 128
HG_DV = 128
HG_KW = HG_HEADS * HG_DK
HG_WIDTH = HG_HEADS * HG_DV
NORM_EPS = 1e-6
IN_SPLITS = (Q_LORA, KV_LORA, MLA_ROPE, HG_KW, HG_KW, HG_WIDTH, HG_WIDTH, D_MODEL, D_MODEL)

LANES = 128
HEAD_PAD = LANES
ATTN_GROUP = 4
HG_CHUNK = 32
HG_SUB = 256
HG_SAFE_DECAY = 60.0
LOG2_E = 1.4426950408889634
MLA_IN_W = 896
MERGE_SPLIT = 2
FFN_SPLIT = 11
VMEM_LIMIT = 56 * 1024 * 1024

BF16 = jnp.bfloat16
F32 = jnp.float32
NT = (((1,), (1,)), ((), ()))
TN = (((0,), (0,)), ((), ()))


def _rms(x, g):
    return x * lax.rsqrt(jnp.mean(x * x, axis=-1, keepdims=True) + NORM_EPS) * g


def _dot(a, b):
    return jnp.dot(a, b, preferred_element_type=F32)


def _const_spec(shape):
    nd = len(shape)
    return pl.BlockSpec(shape, lambda *_: (0,) * nd, pipeline_mode=pl.Buffered(1))


def _params(sem):
    return pltpu.CompilerParams(dimension_semantics=sem, vmem_limit_bytes=VMEM_LIMIT)


def _attn_kernel(qt_ref, k_ref, vt_ref, o_ref, s_scr, m_scr, *, tq):
    i = pl.program_id(2)
    nq = pl.num_programs(2)
    kv_idx = lax.broadcasted_iota(jnp.int32, (tq, tq), 0)
    q_idx = lax.broadcasted_iota(jnp.int32, (tq, tq), 1)
    causal = q_idx >= kv_idx

    def scores(j, iq):
        start = pl.multiple_of(j * tq, tq)
        return tuple(_dot(k_ref[pl.ds(start, tq), hh * HEAD_PAD:(hh + 1) * HEAD_PAD],
                          qt_ref[iq, hh * HEAD_PAD:(hh + 1) * HEAD_PAD, :]) for hh in range(ATTN_GROUP))

    def stash(slot, s_all):
        for hh in range(ATTN_GROUP):
            s_scr[slot, hh] = s_all[hh]
            m_scr[slot, hh] = jnp.max(s_all[hh], axis=0, keepdims=True)

    def update(j, slot, carry, masked):
        vt = vt_ref[j]
        out = []
        for hh in range(ATTN_GROUP):
            m, l, acc = carry[hh]
            s = s_scr[slot, hh]
            if masked:
                s = jnp.where(causal, s, -jnp.inf)
                s_max = jnp.max(s, axis=0, keepdims=True)
            else:
                s_max = m_scr[slot, hh]
            m_new = jnp.maximum(m, s_max)
            alpha = jnp.exp2(m - m_new)
            p = jnp.exp2(s - m_new)
            l = alpha * l + jnp.sum(p, axis=0, keepdims=True)
            acc = alpha * acc + _dot(vt[hh * MLA_V:(hh + 1) * MLA_V, :], p.astype(BF16))
            out.append((m_new, l, acc))
        return tuple(out)

    @pl.when(i == 0)
    def _():
        stash(0, scores(0, 0))

    init = tuple((jnp.full((1, tq), -jnp.inf, F32), jnp.zeros((1, tq), F32), jnp.zeros((MLA_V, tq), F32))
                 for _ in range(ATTN_GROUP))

    def step(t, src, dst, carry):
        s_next = scores(t + 1, i)
        carry = update(t, src, carry, False)
        stash(dst, s_next)
        return carry

    def pair(jj, carry):
        return step(2 * jj + 1, 1, 0, step(2 * jj, 0, 1, carry))

    carry = lax.fori_loop(0, i // 2, pair, init)
    carry = lax.cond(lax.rem(i, 2) == 1, lambda c: step(i - 1, 0, 0, c), lambda c: c, carry)
    s_ahead = scores(0, jnp.minimum(i + 1, nq - 1))
    carry = update(i, 0, carry, True)
    stash(0, s_ahead)
    o_t = jnp.concatenate([acc / l for _, l, acc in carry], axis=0)
    o_ref[...] = o_t.T.astype(BF16)


def _mla_attn(qt, k, vt, batch, seq, tq):
    nq = seq // tq
    return pl.pallas_call(
        functools.partial(_attn_kernel, tq=tq),
        grid=(batch, MLA_HEADS // ATTN_GROUP, nq),
        in_specs=[
            pl.BlockSpec((None, nq, ATTN_GROUP * HEAD_PAD, tq), lambda b, g, i: (b, 0, g, 0)),
            pl.BlockSpec((seq, ATTN_GROUP * HEAD_PAD), lambda b, g, i: (b, g)),
            pl.BlockSpec((None, None, nq, ATTN_GROUP * MLA_V, tq), lambda b, g, i: (b, g, 0, 0, 0)),
        ],
        out_specs=pl.BlockSpec((tq, ATTN_GROUP * MLA_V), lambda b, g, i: (b * nq + i, g)),
        out_shape=jax.ShapeDtypeStruct((batch * seq, MLA_HEADS * MLA_V), BF16),
        scratch_shapes=[
            pltpu.VMEM((2, ATTN_GROUP, tq, tq), F32),
            pltpu.VMEM((2, ATTN_GROUP, 1, tq), F32),
        ],
        compiler_params=_params(("parallel", "parallel", "arbitrary")),
        name="mla_attn",
    )(qt, k, vt)


def _split3(x):
    a = x.astype(BF16)
    r = x - a.astype(F32)
    b = r.astype(BF16)
    c = (r - b.astype(F32)).astype(BF16)
    return a, b, c


def _mixer_in_kernel(x_ref, gmix_ref, wcat_ref, gcq_ref, wuq_ref, gckv_ref, wukv_ref, nmask_ref,
                     tq_ref, ck_ref, sk_ref, loglb_ref, log1mlb_ref, gout_ref, tri_ref, cmask_ref,
                     qt_ref, k_ref, vt_ref, o_ref,
                     st_ref, stbak_ref, q_scr, k_scr, v_scr, b_scr, o_scr, g_scr, *, tm):
    S = HG_SUB
    C = HG_CHUNK
    n_sub = tm // S
    ncs = S // C
    nc = tm // C

    @pl.when(pl.program_id(1) == 0)
    def _():
        st_ref[...] = jnp.zeros_like(st_ref)

    stbak_ref[...] = st_ref[...]
    gmix = gmix_ref[...]
    zzs = [_dot(_rms(x_ref[u * S:(u + 1) * S, :], gmix).astype(BF16), wcat_ref[...]) for u in range(n_sub)]
    zs = [zz[:, MLA_IN_W:] for zz in zzs]
    nmask = nmask_ref[...]
    inv_qk = 1.0 / MLA_QK
    gw = ATTN_GROUP * MLA_V

    tri = tri_ref[...]
    cmask = cmask_ref[...]
    row = lax.broadcasted_iota(jnp.int32, (S, S), 0)
    col = lax.broadcasted_iota(jnp.int32, (S, S), 1)
    causal = (row >= col) & (row // C == col // C)
    a = loglb_ref[...]
    log1mlb = log1mlb_ref[...]
    sts = [st_ref[hd] for hd in range(HG_HEADS)]
    b_mins = []
    for u in range(n_sub):
        rows = slice(u * S, (u + 1) * S)
        zc = zzs[u][:, :MLA_IN_W]
        c_q = zc[:, :Q_LORA]
        c_kv = zc[:, Q_LORA:Q_LORA + KV_LORA]
        kpe_a = zc[:, 640:768]
        kpe_b = zc[:, 768:896]
        q2 = _dot(_rms(c_q, gcq_ref[...]).astype(BF16), wuq_ref[...])
        kv2 = _dot(_rms(c_kv, gckv_ref[...]).astype(BF16), wukv_ref[...])
        tq_tab, ck, sk = tq_ref[rows, :], ck_ref[rows, :], sk_ref[rows, :]
        for hd in range(MLA_HEADS):
            lo = hd * HEAD_PAD
            qa = q2[:, lo:lo + HEAD_PAD]
            rq = lax.rsqrt(jnp.sum(qa * qa * nmask, axis=-1, keepdims=True) * inv_qk + NORM_EPS)
            qt_ref[u, lo:lo + HEAD_PAD, :] = (qa * tq_tab * rq).astype(BF16).T
            ka = kv2[:, lo:lo + HEAD_PAD] + kpe_a
            rk = lax.rsqrt(jnp.sum(ka * ka * nmask, axis=-1, keepdims=True) * inv_qk + NORM_EPS)
            k_ref[rows, lo:lo + HEAD_PAD] = ((ka * ck + kpe_b * sk) * rk).astype(BF16)
        v = kv2[:, MLA_HEADS * HEAD_PAD:]
        for g in range(MLA_HEADS // ATTN_GROUP):
            vt_ref[g, u] = v[:, g * gw:(g + 1) * gw].astype(BF16).T

        z = zs[u]
        zq = z[:, :HG_KW]
        zf = z[:, HG_KW:2 * HG_KW]
        zi = z[:, 2 * HG_KW:2 * HG_KW + HG_WIDTH]
        zg = z[:, 2 * HG_KW + HG_WIDTH:]
        log_sig = jnp.minimum(zf, 0.0) - jnp.log(1.0 + jnp.exp(-jnp.abs(zf)))
        c = log1mlb + log_sig
        log_f = jnp.maximum(a, c) + jnp.log(1.0 + jnp.exp(-jnp.abs(a - c)))
        kk = 1.0 - jnp.exp(log_f)
        l1, l2, l3 = _split3(log_f)
        b = _dot(tri, l1) + _dot(tri, l2) + _dot(tri, l3)
        b3 = b.reshape(ncs, C, HG_KW)
        b_last = b3[:, C - 1:C, :]
        q_scr[rows, :] = zq
        k_scr[rows, :] = kk
        v_scr[rows, :] = zi
        b_scr[rows, :] = b
        g_scr[rows, :] = zg * jax.nn.sigmoid(zg)
        b_mins.append(jnp.min(b))

        qe = (zq * jnp.exp(b)).astype(BF16)
        ke = (kk * jnp.exp(jnp.minimum(-b, HG_SAFE_DECAY + 20.0))).astype(BF16)
        kl = (kk.reshape(ncs, C, HG_KW) * jnp.exp(b_last - b3)).reshape(S, HG_KW).astype(BF16)
        e_last = jnp.exp(b_last)
        vb = zi.astype(BF16)
        vt = vb.T
        for hd in range(HG_HEADS):
            lo = hd * HG_DK
            qe_h = qe[:, lo:lo + HG_DK]
            att = lax.dot_general(qe_h, ke[:, lo:lo + HG_DK], NT, preferred_element_type=F32)
            att = jnp.where(causal, att, 0.0).astype(BF16)
            o_intra = _dot(att, vb[:, lo:lo + HG_DV])
            vt_h = vt[lo:lo + HG_DV, :]
            upd = _dot(jnp.concatenate([vt_h] * ncs, axis=0) * cmask, kl[:, lo:lo + HG_DK])
            st = sts[hd]
            for ci in range(ncs):
                r0 = ci * C
                o_inter = lax.dot_general(qe_h[r0:r0 + C], st.astype(BF16), NT, preferred_element_type=F32)
                o_scr[u * S + r0:u * S + r0 + C, lo:lo + HG_DV] = o_intra[r0:r0 + C] + o_inter
                st = st * e_last[ci, :, lo:lo + HG_DK] + upd[ci * HG_DV:(ci + 1) * HG_DV]
            sts[hd] = st
    for hd in range(HG_HEADS):
        st_ref[hd] = sts[hd]
    b_min = b_mins[0]
    for u in range(1, n_sub):
        b_min = jnp.minimum(b_min, b_mins[u])
    unsafe = b_min < -HG_SAFE_DECAY

    @pl.when(unsafe)
    def _():
        st_ref[...] = stbak_ref[...]
        t_idx = lax.broadcasted_iota(jnp.int32, (C, 1), 0)
        for hd in range(HG_HEADS):
            lo = hd * HG_DK

            def chunk_body(ci, _, lo=lo, hd=hd):
                r0 = pl.multiple_of(ci * C, C)
                q_c = q_scr[pl.ds(r0, C), lo:lo + HG_DK]
                k_c = k_scr[pl.ds(r0, C), lo:lo + HG_DK]
                v_c = v_scr[pl.ds(r0, C), lo:lo + HG_DV]
                b_c = b_scr[pl.ds(r0, C), lo:lo + HG_DK]
                bl = b_c[C - 1:C, :]

                def s_body(s, o_acc):
                    pick = t_idx == s
                    ks = jnp.sum(jnp.where(pick, k_c, 0.0), axis=0, keepdims=True)
                    bs = jnp.sum(jnp.where(pick, b_c, 0.0), axis=0, keepdims=True)
                    vs = jnp.sum(jnp.where(pick, v_c, 0.0), axis=0, keepdims=True)
                    dec = jnp.exp(jnp.minimum(b_c - bs, 0.0))
                    w = jnp.sum(q_c * ks * dec, axis=-1, keepdims=True)
                    w = jnp.where(t_idx >= s, w, 0.0)
                    return o_acc + w * vs

                o_intra = lax.fori_loop(0, C, s_body, jnp.zeros((C, HG_DV), F32))
                st = st_ref[hd]
                qe_c = (q_c * jnp.exp(b_c)).astype(BF16)
                o_inter = lax.dot_general(qe_c, st.astype(BF16), NT, preferred_element_type=F32)
                o_scr[pl.ds(r0, C), lo:lo + HG_DV] = o_intra + o_inter
                kl_c = (k_c * jnp.exp(bl - b_c)).astype(BF16)
                upd = lax.dot_general(v_c.astype(BF16), kl_c, TN, preferred_element_type=F32)
                st_ref[hd] = st * jnp.exp(bl) + upd
                return 0

            lax.fori_loop(0, nc, chunk_body, 0)

    o = o_scr[...]
    gate = g_scr[...]
    gout = gout_ref[...]
    for hd in range(HG_HEADS):
        lo = hd * HG_DV
        o_ref[:, lo:lo + HG_DV] = (_rms(o[:, lo:lo + HG_DV], gout) * gate[:, lo:lo + HG_DV]).astype(BF16)


def _mixer_in(x2, gmix, wcat, gcq, wuq, gckv, wukv, nmask, tabs, loglb, log1mlb, gout, tri, cmask,
              batch, seq, tm):
    nt = seq // tm
    n_sub = tm // HG_SUB
    n = batch * seq
    tab_spec = pl.BlockSpec((tm, LANES), lambda b, i: (i, 0))

    def row_spec(width):
        return pl.BlockSpec((tm, width), lambda b, i: (b * nt + i, 0))

    return pl.pallas_call(
        functools.partial(_mixer_in_kernel, tm=tm),
        grid=(batch, nt),
        in_specs=[
            row_spec(D_MODEL),
            _const_spec(gmix.shape), _const_spec(wcat.shape), _const_spec(gcq.shape),
            _const_spec(wuq.shape), _const_spec(gckv.shape), _const_spec(wukv.shape),
            _const_spec(nmask.shape), tab_spec, tab_spec, tab_spec,
            _const_spec(loglb.shape), _const_spec(log1mlb.shape), _const_spec(gout.shape),
            _const_spec(tri.shape), _const_spec(cmask.shape),
        ],
        out_specs=[
            pl.BlockSpec((None, n_sub, MLA_HEADS * HEAD_PAD, HG_SUB), lambda b, i: (b, i, 0, 0)),
            row_spec(MLA_HEADS * HEAD_PAD),
            pl.BlockSpec((None, MLA_HEADS // ATTN_GROUP, n_sub, ATTN_GROUP * MLA_V, HG_SUB),
                         lambda b, i: (b, 0, i, 0, 0)),
            row_spec(HG_WIDTH),
        ],
        out_shape=[
            jax.ShapeDtypeStruct((batch, seq // HG_SUB, MLA_HEADS * HEAD_PAD, HG_SUB), BF16),
            jax.ShapeDtypeStruct((n, MLA_HEADS * HEAD_PAD), BF16),
            jax.ShapeDtypeStruct((batch, MLA_HEADS // ATTN_GROUP, seq // HG_SUB, ATTN_GROUP * MLA_V, HG_SUB), BF16),
            jax.ShapeDtypeStruct((n, HG_WIDTH), BF16),
        ],
        scratch_shapes=[
            pltpu.VMEM((HG_HEADS, HG_DV, HG_DK), F32),
            pltpu.VMEM((HG_HEADS, HG_DV, HG_DK), F32),
            pltpu.VMEM((tm, HG_KW), F32),
            pltpu.VMEM((tm, HG_KW), F32),
            pltpu.VMEM((tm, HG_WIDTH), F32),
            pltpu.VMEM((tm, HG_KW), F32),
            pltpu.VMEM((tm, HG_WIDTH), F32),
            pltpu.VMEM((tm, HG_WIDTH), F32),
        ],
        compiler_params=_params(("parallel", "arbitrary")),
        name="mixer_in",
    )(x2, gmix, wcat, gcq, wuq, gckv, wukv, nmask, *tabs, loglb, log1mlb, gout, tri, cmask)


def _merge_kernel(x_ref, ya_ref, yb_ref, gmix_ref, wg_ref, wpa_ref, wpb_ref, wout_ref, o_ref):
    x = x_ref[...]
    h = _rms(x, gmix_ref[...]).astype(BF16)
    ya, yb = ya_ref[...], yb_ref[...]
    acc = x
    w = D_MODEL // MERGE_SPLIT
    for j in range(MERGE_SPLIT):
        cols = slice(j * w, (j + 1) * w)
        g_a = _dot(h, wg_ref[:, cols])
        g_b = _dot(h, wg_ref[:, D_MODEL + j * w:D_MODEL + (j + 1) * w])
        y = (jax.nn.sigmoid(g_a) * _dot(ya, wpa_ref[:, cols])
             + jax.nn.sigmoid(g_b) * _dot(yb, wpb_ref[:, cols]))
        acc = acc + _dot(y.astype(BF16), wout_ref[cols, :])
    o_ref[...] = acc


def _merge(x2, ya, yb, gmix, wg, wpa, wpb, wout, tm):
    n = x2.shape[0]
    return pl.pallas_call(
        _merge_kernel,
        grid=(n // tm,),
        in_specs=[
            pl.BlockSpec((tm, D_MODEL), lambda i: (i, 0)),
            pl.BlockSpec((tm, ya.shape[1]), lambda i: (i, 0)),
            pl.BlockSpec((tm, yb.shape[1]), lambda i: (i, 0)),
            _const_spec(gmix.shape), _const_spec(wg.shape), _const_spec(wpa.shape),
            _const_spec(wpb.shape), _const_spec(wout.shape),
        ],
        out_specs=pl.BlockSpec((tm, D_MODEL), lambda i: (i, 0)),
        out_shape=jax.ShapeDtypeStruct((n, D_MODEL), F32),
        compiler_params=_params(("parallel",)),
        name="merge",
    )(x2, ya, yb, gmix, wg, wpa, wpb, wout)


def _ffn_kernel(x_ref, gffn_ref, wgate_ref, wup_ref, wdown_ref, o_ref, *, n_split):
    x = x_ref[...]
    h = _rms(x, gffn_ref[...]).astype(BF16)
    d_ff = wgate_ref.shape[1]
    w = d_ff // n_split
    acc = x
    for j in range(n_split):
        gate = _dot(h, wgate_ref[:, j * w:(j + 1) * w])
        up = _dot(h, wup_ref[:, j * w:(j + 1) * w])
        act = (gate * jax.nn.sigmoid(gate) * up).astype(BF16)
        acc = acc + _dot(act, wdown_ref[j * w:(j + 1) * w, :])
    o_ref[...] = acc


def _ffn(x2, gffn, wgate, wup, wdown, tm, n_split):
    n = x2.shape[0]
    return pl.pallas_call(
        functools.partial(_ffn_kernel, n_split=n_split),
        grid=(n // tm,),
        in_specs=[
            pl.BlockSpec((tm, D_MODEL), lambda i: (i, 0)),
            _const_spec(gffn.shape), _const_spec(wgate.shape), _const_spec(wup.shape),
            _const_spec(wdown.shape),
        ],
        out_specs=pl.BlockSpec((tm, D_MODEL), lambda i: (i, 0)),
        out_shape=jax.ShapeDtypeStruct((n, D_MODEL), F32),
        compiler_params=_params(("parallel",)),
        name="ffn",
    )(x2, gffn, wgate, wup, wdown)


def _rope_tables(seq, gain, scale):
    pos = jnp.arange(seq, dtype=F32)
    inv_freq = 1.0 / (ROPE_THETA ** (jnp.arange(0, MLA_ROPE, 2, dtype=F32) / MLA_ROPE))
    ang = pos[:, None] * inv_freq[None, :]
    cos, sin = jnp.cos(ang), jnp.sin(ang)
    half = MLA_ROPE // 2
    g_nope, g_rope = gain[:MLA_NOPE], gain[MLA_NOPE:]
    nope = jnp.broadcast_to(g_nope[None, :], (seq, MLA_NOPE))
    c_rope = jnp.concatenate([cos, cos], axis=1) * g_rope[None, :]
    g_partner = jnp.concatenate([g_rope[half:], g_rope[:half]])
    s_rope = jnp.concatenate([-sin, sin], axis=1) * g_partner[None, :]
    zeros = jnp.zeros((seq, MLA_NOPE), F32)
    q_tab = jnp.concatenate([nope, c_rope, s_rope], axis=1) * scale
    ctab = jnp.concatenate([nope, c_rope, c_rope], axis=1) * scale
    stab = jnp.concatenate([zeros, s_rope, s_rope], axis=1) * scale
    return q_tab, ctab, stab


def _layer_params(l, seq, tm_hg, w_in, mla_w_uq, mla_w_ukv, mla_q_norm, mla_k_norm, lower_bounds):
    o = [0]
    for s in IN_SPLITS:
        o.append(o[-1] + s)
    wi = w_in[l]
    half = MLA_ROPE // 2
    z64 = jnp.zeros((D_MODEL, MLA_NOPE), F32)
    w_kpe = wi[:, o[2]:o[3]]
    w_kpe_sw = jnp.concatenate([w_kpe[:, half:], w_kpe[:, :half]], axis=1)
    wmla = jnp.concatenate([wi[:, o[0]:o[2]], z64, w_kpe, w_kpe, z64, w_kpe_sw, w_kpe_sw], axis=1).astype(BF16)
    whg = wi[:, o[3]:o[7]].astype(BF16)
    wgates = wi[:, o[7]:o[9]].astype(BF16)

    uq = mla_w_uq[l].reshape(Q_LORA, MLA_HEADS, MLA_QK)
    uq_nope, uq_rope = uq[..., :MLA_NOPE], uq[..., MLA_NOPE:]
    uq_rope_sw = jnp.concatenate([uq_rope[..., half:], uq_rope[..., :half]], axis=-1)
    wuq = jnp.concatenate([uq_nope, uq_rope, uq_rope_sw], axis=-1).reshape(
        Q_LORA, MLA_HEADS * HEAD_PAD).astype(BF16)

    ukv = mla_w_ukv[l].reshape(KV_LORA, MLA_HEADS, MLA_NOPE + MLA_V)
    zk64 = jnp.zeros((KV_LORA, MLA_HEADS, HEAD_PAD - MLA_NOPE), F32)
    wuk = jnp.concatenate([ukv[..., :MLA_NOPE], zk64], axis=-1).reshape(KV_LORA, MLA_HEADS * HEAD_PAD)
    wuv = ukv[..., MLA_NOPE:].reshape(KV_LORA, MLA_HEADS * MLA_V)
    wukv = jnp.concatenate([wuk, wuv], axis=1).astype(BF16)

    tq_tab, _, _ = _rope_tables(seq, mla_q_norm[l], MLA_QK ** -0.5 * LOG2_E)
    _, ck, sk = _rope_tables(seq, mla_k_norm[l], 1.0)
    norm_mask = (jnp.arange(HEAD_PAD) < MLA_QK).astype(F32)[None, :]

    lb = lower_bounds[l][None, :]
    r = jnp.arange(tm_hg)
    tri = ((r[:, None] // HG_CHUNK == r[None, :] // HG_CHUNK) & (r[:, None] >= r[None, :])).astype(BF16)
    nc = tm_hg // HG_CHUNK
    cmask = (jnp.arange(nc * HG_DV)[:, None] // HG_DV == r[None, :] // HG_CHUNK).astype(BF16)
    return dict(wcat=jnp.concatenate([wmla, whg], axis=1), wgates=wgates, wuq=wuq, wukv=wukv,
                tabs=(tq_tab, ck, sk), norm_mask=norm_mask,
                loglb=jnp.log(lb), log1mlb=jnp.log1p(-lb), tri=tri, cmask=cmask)


def kernel(x, norm_mix, w_in, mla_norm_cq, mla_w_uq, mla_norm_ckv, mla_w_ukv, mla_q_norm, mla_k_norm,
           hg_lb_logits, hg_out_norm, w_proj_a, w_proj_b, w_out, norm_ffn, w_gate, w_up, w_down):
    batch, seq, d = x.shape
    assert d == D_MODEL
    depth = w_in.shape[0]
    tm = min(HG_SUB, seq)
    tm_hg = min(2 * HG_SUB, seq)
    tm_ffn = min(1024, seq)
    assert all(seq % t == 0 for t in (tm, tm_hg, tm_ffn))

    p = jax.nn.softmax(hg_lb_logits.astype(F32), axis=0)
    lower_bounds = jnp.maximum(jnp.cumsum(p, axis=0) - p[0:1], 0.0)

    x2 = x.reshape(batch * seq, d)
    for l in range(depth):
        lp = _layer_params(l, seq, min(HG_SUB, seq), w_in, mla_w_uq, mla_w_ukv, mla_q_norm, mla_k_norm, lower_bounds)
        gmix = norm_mix[l][None, :]
        qt, k, vt, y_b = _mixer_in(x2, gmix, lp["wcat"], mla_norm_cq[l][None, :], lp["wuq"],
                                   mla_norm_ckv[l][None, :], lp["wukv"], lp["norm_mask"], lp["tabs"],
                                   lp["loglb"], lp["log1mlb"], hg_out_norm[l][None, :], lp["tri"], lp["cmask"],
                                   batch, seq, tm_hg)
        y_a = _mla_attn(qt, k, vt, batch, seq, tm)
        x2 = _merge(x2, y_a, y_b, gmix, lp["wgates"], w_proj_a[l].astype(BF16),
                    w_proj_b[l].astype(BF16), w_out[l].astype(BF16), tm_ffn)
        x2 = _ffn(x2, norm_ffn[l][None, :], w_gate[l].astype(BF16), w_up[l].astype(BF16),
                  w_down[l].astype(BF16), tm_ffn, FFN_SPLIT)
    return x2.reshape(batch, seq, d)
```

```python
import functools

import jax
import jax.numpy as jnp
from jax import lax
from jax.experimental import pallas as pl
from jax.experimental.pallas import tpu as pltpu

D_MODEL = 1024
MLA_HEADS = 8
MLA_NOPE = 64
MLA_ROPE = 32
MLA_QK = MLA_NOPE + MLA_ROPE
MLA_V = 64
Q_LORA = 384
KV_LORA = 256
ROPE_THETA = 10000.0
HG_HEADS = 4
HG_DK = 128
HG_DV = 128
HG_KW = HG_HEADS * HG_DK
HG_WIDTH = HG_HEADS * HG_DV
NORM_EPS = 1e-6
IN_SPLITS = (Q_LORA, KV_LORA, MLA_ROPE, HG_KW, HG_KW, HG_WIDTH, HG_WIDTH, D_MODEL, D_MODEL)

LANES = 128
HEAD_PAD = LANES
ATTN_GROUP = 4
L_ROWS = 16
HG_CHUNK = 32
HG_SUB = 256
HG_SAFE_DECAY = 60.0
LOG2_E = 1.4426950408889634
MLA_IN_W = 896
MERGE_SPLIT = 2
FFN_SPLIT = 11
VMEM_LIMIT = 56 * 1024 * 1024

BF16 = jnp.bfloat16
F32 = jnp.float32
NT = (((1,), (1,)), ((), ()))
TN = (((0,), (0,)), ((), ()))


def _rms(x, g):
    return x * lax.rsqrt(jnp.mean(x * x, axis=-1, keepdims=True) + NORM_EPS) * g


def _dot(a, b):
    return jnp.dot(a, b, preferred_element_type=F32)


def _const_spec(shape):
    nd = len(shape)
    return pl.BlockSpec(shape, lambda *_: (0,) * nd, pipeline_mode=pl.Buffered(1))


def _params(sem):
    return pltpu.CompilerParams(dimension_semantics=sem, vmem_limit_bytes=VMEM_LIMIT)


def _attn_kernel(qt_ref, k_ref, vt_ref, o_ref, s_scr, m_scr, *, tq):
    i = pl.program_id(2)
    nq = pl.num_programs(2)
    kv_idx = lax.broadcasted_iota(jnp.int32, (tq, tq), 0)
    q_idx = lax.broadcasted_iota(jnp.int32, (tq, tq), 1)
    causal = q_idx >= kv_idx

    def scores(j, iq):
        start = pl.multiple_of(j * tq, tq)
        return tuple(_dot(k_ref[pl.ds(start, tq), hh * HEAD_PAD:(hh + 1) * HEAD_PAD],
                          qt_ref[iq, hh * HEAD_PAD:(hh + 1) * HEAD_PAD, :]) for hh in range(ATTN_GROUP))

    def stash(slot, s_all):
        for hh in range(ATTN_GROUP):
            s_scr[slot, hh] = s_all[hh]
            m_scr[slot, hh] = jnp.max(s_all[hh], axis=0, keepdims=True)

    ones_rows = jnp.ones((L_ROWS, tq), BF16)

    def update(j, slot, carry, masked):
        vt = vt_ref[j]
        out = []
        for hh in range(ATTN_GROUP):
            m, accl = carry[hh]
            s = s_scr[slot, hh]
            if masked:
                s = jnp.where(causal, s, -jnp.inf)
                s_max = jnp.max(s, axis=0, keepdims=True)
            else:
                s_max = m_scr[slot, hh]
            m_new = jnp.maximum(m, s_max)
            alpha = jnp.exp2(m - m_new)
            p = jnp.exp2(s - m_new).astype(BF16)
            lhs = jnp.concatenate([vt[hh * MLA_V:(hh + 1) * MLA_V, :], ones_rows], axis=0)
            out.append((m_new, alpha * accl + _dot(lhs, p)))
        return tuple(out)

    @pl.when(i == 0)
    def _():
        stash(0, scores(0, 0))

    init = tuple((jnp.full((1, tq), -jnp.inf, F32), jnp.zeros((MLA_V + L_ROWS, tq), F32))
                 for _ in range(ATTN_GROUP))

    def step(t, src, dst, carry):
        s_next = scores(t + 1, i)
        carry = update(t, src, carry, False)
        stash(dst, s_next)
        return carry

    def pair(jj, carry):
        return step(2 * jj + 1, 1, 0, step(2 * jj, 0, 1, carry))

    def tail(carry, odd):
        slot = 0
        if odd:
            carry = step(i - 1, 0, 1, carry)
            slot = 1
        s_ahead = scores(0, jnp.minimum(i + 1, nq - 1))
        carry = update(i, slot, carry, True)
        stash(0, s_ahead)
        return carry

    carry = lax.fori_loop(0, i // 2, pair, init)
    carry = lax.cond(lax.rem(i, 2) == 1, functools.partial(tail, odd=True), functools.partial(tail, odd=False),
                     carry)
    o_t = jnp.concatenate([accl[:MLA_V] / accl[MLA_V:MLA_V + 1] for _, accl in carry], axis=0)
    o_ref[...] = o_t.T.astype(BF16)


def _mla_attn(qt, k, vt, batch, seq, tq):
    nq = seq // tq
    return pl.pallas_call(
        functools.partial(_attn_kernel, tq=tq),
        grid=(batch, MLA_HEADS // ATTN_GROUP, nq),
        in_specs=[
            pl.BlockSpec((None, nq, ATTN_GROUP * HEAD_PAD, tq), lambda b, g, i: (b, 0, g, 0)),
            pl.BlockSpec((seq, ATTN_GROUP * HEAD_PAD), lambda b, g, i: (b, g)),
            pl.BlockSpec((None, None, nq, ATTN_GROUP * MLA_V, tq), lambda b, g, i: (b, g, 0, 0, 0)),
        ],
        out_specs=pl.BlockSpec((tq, ATTN_GROUP * MLA_V), lambda b, g, i: (b * nq + i, g)),
        out_shape=jax.ShapeDtypeStruct((batch * seq, MLA_HEADS * MLA_V), BF16),
        scratch_shapes=[
            pltpu.VMEM((2, ATTN_GROUP, tq, tq), F32),
            pltpu.VMEM((2, ATTN_GROUP, 1, tq), F32),
        ],
        compiler_params=_params(("parallel", "parallel", "arbitrary")),
        name="mla_attn",
    )(qt, k, vt)


def _split3(x):
    a = x.astype(BF16)
    r = x - a.astype(F32)
    b = r.astype(BF16)
    c = (r - b.astype(F32)).astype(BF16)
    return a, b, c


def _mixer_in_kernel(x_ref, gmix_ref, wcat_ref, gcq_ref, wuq_ref, gckv_ref, wukv_ref, nmask_ref,
                     tq_ref, ck_ref, sk_ref, loglb_ref, log1mlb_ref, gout_ref, tri_ref, cmask_ref,
                     qt_ref, k_ref, vt_ref, o_ref,
                     st_ref, stbak_ref, q_scr, k_scr, v_scr, b_scr, o_scr, g_scr, *, tm):
    S = HG_SUB
    C = HG_CHUNK
    n_sub = tm // S
    ncs = S // C
    nc = tm // C

    @pl.when(pl.program_id(1) == 0)
    def _():
        st_ref[...] = jnp.zeros_like(st_ref)

    stbak_ref[...] = st_ref[...]
    gmix = gmix_ref[...]
    zzs = [_dot(_rms(x_ref[u * S:(u + 1) * S, :], gmix).astype(BF16), wcat_ref[...]) for u in range(n_sub)]
    zs = [zz[:, MLA_IN_W:] for zz in zzs]
    nmask = nmask_ref[...]
    inv_qk = 1.0 / MLA_QK
    gw = ATTN_GROUP * MLA_V

    tri = tri_ref[...]
    cmask = cmask_ref[...]
    row = lax.broadcasted_iota(jnp.int32, (S, S), 0)
    col = lax.broadcasted_iota(jnp.int32, (S, S), 1)
    causal = (row >= col) & (row // C == col // C)
    a = loglb_ref[...]
    log1mlb = log1mlb_ref[...]
    sts = [st_ref[hd] for hd in range(HG_HEADS)]
    b_mins = []
    for u in range(n_sub):
        rows = slice(u * S, (u + 1) * S)
        zc = zzs[u][:, :MLA_IN_W]
        c_q = zc[:, :Q_LORA]
        c_kv = zc[:, Q_LORA:Q_LORA + KV_LORA]
        kpe_a = zc[:, 640:768]
        kpe_b = zc[:, 768:896]
        q2 = _dot(_rms(c_q, gcq_ref[...]).astype(BF16), wuq_ref[...])
        kv2 = _dot(_rms(c_kv, gckv_ref[...]).astype(BF16), wukv_ref[...])
        tq_tab, ck, sk = tq_ref[rows, :], ck_ref[rows, :], sk_ref[rows, :]
        for hd in range(MLA_HEADS):
            lo = hd * HEAD_PAD
            qa = q2[:, lo:lo + HEAD_PAD]
            rq = lax.rsqrt(jnp.sum(qa * qa * nmask, axis=-1, keepdims=True) * inv_qk + NORM_EPS)
            qt_ref[u, lo:lo + HEAD_PAD, :] = (qa * tq_tab * rq).astype(BF16).T
            ka = kv2[:, lo:lo + HEAD_PAD] + kpe_a
            rk = lax.rsqrt(jnp.sum(ka * ka * nmask, axis=-1, keepdims=True) * inv_qk + NORM_EPS)
            k_ref[rows, lo:lo + HEAD_PAD] = ((ka * ck + kpe_b * sk) * rk).astype(BF16)
        v = kv2[:, MLA_HEADS * HEAD_PAD:]
        for g in range(MLA_HEADS // ATTN_GROUP):
            vt_ref[g, u] = v[:, g * gw:(g + 1) * gw].astype(BF16).T

        z = zs[u]
        zq = z[:, :HG_KW]
        zf = z[:, HG_KW:2 * HG_KW]
        zi = z[:, 2 * HG_KW:2 * HG_KW + HG_WIDTH]
        zg = z[:, 2 * HG_KW + HG_WIDTH:]
        log_sig = jnp.minimum(zf, 0.0) - jnp.log(1.0 + jnp.exp(-jnp.abs(zf)))
        c = log1mlb + log_sig
        log_f = jnp.maximum(a, c) + jnp.log(1.0 + jnp.exp(-jnp.abs(a - c)))
        kk = 1.0 - jnp.exp(log_f)
        l1, l2, l3 = _split3(log_f)
        b = _dot(tri, l1) + _dot(tri, l2) + _dot(tri, l3)
        b3 = b.reshape(ncs, C, HG_KW)
        b_last = b3[:, C - 1:C, :]
        q_scr[rows, :] = zq
        k_scr[rows, :] = kk
        v_scr[rows, :] = zi
        b_scr[rows, :] = b
        g_scr[rows, :] = zg * jax.nn.sigmoid(zg)
        b_mins.append(jnp.min(b))

        qe = (zq * jnp.exp(b)).astype(BF16)
        ke = (kk * jnp.exp(jnp.minimum(-b, HG_SAFE_DECAY + 20.0))).astype(BF16)
        kl = (kk.reshape(ncs, C, HG_KW) * jnp.exp(b_last - b3)).reshape(S, HG_KW).astype(BF16)
        e_last = jnp.exp(b_last)
        vb = zi.astype(BF16)
        vt = vb.T
        for hd in range(HG_HEADS):
            lo = hd * HG_DK
            qe_h = qe[:, lo:lo + HG_DK]
            att = lax.dot_general(qe_h, ke[:, lo:lo + HG_DK], NT, preferred_element_type=F32)
            att = jnp.where(causal, att, 0.0).astype(BF16)
            o_intra = _dot(att, vb[:, lo:lo + HG_DV])
            vt_h = vt[lo:lo + HG_DV, :]
            upd = _dot(jnp.concatenate([vt_h] * ncs, axis=0) * cmask, kl[:, lo:lo + HG_DK])
            st = sts[hd]
            for ci in range(ncs):
                r0 = ci * C
                o_inter = lax.dot_general(qe_h[r0:r0 + C], st.astype(BF16), NT, preferred_element_type=F32)
                o_scr[u * S + r0:u * S + r0 + C, lo:lo + HG_DV] = o_intra[r0:r0 + C] + o_inter
                st = st * e_last[ci, :, lo:lo + HG_DK] + upd[ci * HG_DV:(ci + 1) * HG_DV]
            sts[hd] = st
    for hd in range(HG_HEADS):
        st_ref[hd] = sts[hd]
    b_min = b_mins[0]
    for u in range(1, n_sub):
        b_min = jnp.minimum(b_min, b_mins[u])
    unsafe = b_min < -HG_SAFE_DECAY

    @pl.when(unsafe)
    def _():
        st_ref[...] = stbak_ref[...]
        t_idx = lax.broadcasted_iota(jnp.int32, (C, 1), 0)
        for hd in range(HG_HEADS):
            lo = hd * HG_DK

            def chunk_body(ci, _, lo=lo, hd=hd):
                r0 = pl.multiple_of(ci * C, C)
                q_c = q_scr[pl.ds(r0, C), lo:lo + HG_DK]
                k_c = k_scr[pl.ds(r0, C), lo:lo + HG_DK]
                v_c = v_scr[pl.ds(r0, C), lo:lo + HG_DV]
                b_c = b_scr[pl.ds(r0, C), lo:lo + HG_DK]
                bl = b_c[C - 1:C, :]

                def s_body(s, o_acc):
                    pick = t_idx == s
                    ks = jnp.sum(jnp.where(pick, k_c, 0.0), axis=0, keepdims=True)
                    bs = jnp.sum(jnp.where(pick, b_c, 0.0), axis=0, keepdims=True)
                    vs = jnp.sum(jnp.where(pick, v_c, 0.0), axis=0, keepdims=True)
                    dec = jnp.exp(jnp.minimum(b_c - bs, 0.0))
                    w = jnp.sum(q_c * ks * dec, axis=-1, keepdims=True)
                    w = jnp.where(t_idx >= s, w, 0.0)
                    return o_acc + w * vs

                o_intra = lax.fori_loop(0, C, s_body, jnp.zeros((C, HG_DV), F32))
                st = st_ref[hd]
                qe_c = (q_c * jnp.exp(b_c)).astype(BF16)
                o_inter = lax.dot_general(qe_c, st.astype(BF16), NT, preferred_element_type=F32)
                o_scr[pl.ds(r0, C), lo:lo + HG_DV] = o_intra + o_inter
                kl_c = (k_c * jnp.exp(bl - b_c)).astype(BF16)
                upd = lax.dot_general(v_c.astype(BF16), kl_c, TN, preferred_element_type=F32)
                st_ref[hd] = st * jnp.exp(bl) + upd
                return 0

            lax.fori_loop(0, nc, chunk_body, 0)

    o = o_scr[...]
    gate = g_scr[...]
    gout = gout_ref[...]
    for hd in range(HG_HEADS):
        lo = hd * HG_DV
        o_ref[:, lo:lo + HG_DV] = (_rms(o[:, lo:lo + HG_DV], gout) * gate[:, lo:lo + HG_DV]).astype(BF16)


def _mixer_in(x2, gmix, wcat, gcq, wuq, gckv, wukv, nmask, tabs, loglb, log1mlb, gout, tri, cmask,
              batch, seq, tm):
    nt = seq // tm
    n_sub = tm // HG_SUB
    n = batch * seq
    tab_spec = pl.BlockSpec((tm, LANES), lambda b, i: (i, 0))

    def row_spec(width):
        return pl.BlockSpec((tm, width), lambda b, i: (b * nt + i, 0))

    return pl.pallas_call(
        functools.partial(_mixer_in_kernel, tm=tm),
        grid=(batch, nt),
        in_specs=[
            row_spec(D_MODEL),
            _const_spec(gmix.shape), _const_spec(wcat.shape), _const_spec(gcq.shape),
            _const_spec(wuq.shape), _const_spec(gckv.shape), _const_spec(wukv.shape),
            _const_spec(nmask.shape), tab_spec, tab_spec, tab_spec,
            _const_spec(loglb.shape), _const_spec(log1mlb.shape), _const_spec(gout.shape),
            _const_spec(tri.shape), _const_spec(cmask.shape),
        ],
        out_specs=[
            pl.BlockSpec((None, n_sub, MLA_HEADS * HEAD_PAD, HG_SUB), lambda b, i: (b, i, 0, 0)),
            row_spec(MLA_HEADS * HEAD_PAD),
            pl.BlockSpec((None, MLA_HEADS // ATTN_GROUP, n_sub, ATTN_GROUP * MLA_V, HG_SUB),
                         lambda b, i: (b, 0, i, 0, 0)),
            row_spec(HG_WIDTH),
        ],
        out_shape=[
            jax.ShapeDtypeStruct((batch, seq // HG_SUB, MLA_HEADS * HEAD_PAD, HG_SUB), BF16),
            jax.ShapeDtypeStruct((n, MLA_HEADS * HEAD_PAD), BF16),
            jax.ShapeDtypeStruct((batch, MLA_HEADS // ATTN_GROUP, seq // HG_SUB, ATTN_GROUP * MLA_V, HG_SUB), BF16),
            jax.ShapeDtypeStruct((n, HG_WIDTH), BF16),
        ],
        scratch_shapes=[
            pltpu.VMEM((HG_HEADS, HG_DV, HG_DK), F32),
            pltpu.VMEM((HG_HEADS, HG_DV, HG_DK), F32),
            pltpu.VMEM((tm, HG_KW), F32),
            pltpu.VMEM((tm, HG_KW), F32),
            pltpu.VMEM((tm, HG_WIDTH), F32),
            pltpu.VMEM((tm, HG_KW), F32),
            pltpu.VMEM((tm, HG_WIDTH), F32),
            pltpu.VMEM((tm, HG_WIDTH), F32),
        ],
        compiler_params=_params(("parallel", "arbitrary")),
        name="mixer_in",
    )(x2, gmix, wcat, gcq, wuq, gckv, wukv, nmask, *tabs, loglb, log1mlb, gout, tri, cmask)


def _merge_kernel(x_ref, ya_ref, yb_ref, gmix_ref, wg_ref, wpa_ref, wpb_ref, wout_ref, o_ref):
    x = x_ref[...]
    h = _rms(x, gmix_ref[...]).astype(BF16)
    ya, yb = ya_ref[...], yb_ref[...]
    acc = x
    w = D_MODEL // MERGE_SPLIT
    for j in range(MERGE_SPLIT):
        cols = slice(j * w, (j + 1) * w)
        g_a = _dot(h, wg_ref[:, cols])
        g_b = _dot(h, wg_ref[:, D_MODEL + j * w:D_MODEL + (j + 1) * w])
        y = (jax.nn.sigmoid(g_a) * _dot(ya, wpa_ref[:, cols])
             + jax.nn.sigmoid(g_b) * _dot(yb, wpb_ref[:, cols]))
        acc = acc + _dot(y.astype(BF16), wout_ref[cols, :])
    o_ref[...] = acc


def _merge(x2, ya, yb, gmix, wg, wpa, wpb, wout, tm):
    n = x2.shape[0]
    return pl.pallas_call(
        _merge_kernel,
        grid=(n // tm,),
        in_specs=[
            pl.BlockSpec((tm, D_MODEL), lambda i: (i, 0)),
            pl.BlockSpec((tm, ya.shape[1]), lambda i: (i, 0)),
            pl.BlockSpec((tm, yb.shape[1]), lambda i: (i, 0)),
            _const_spec(gmix.shape), _const_spec(wg.shape), _const_spec(wpa.shape),
            _const_spec(wpb.shape), _const_spec(wout.shape),
        ],
        out_specs=pl.BlockSpec((tm, D_MODEL), lambda i: (i, 0)),
        out_shape=jax.ShapeDtypeStruct((n, D_MODEL), F32),
        compiler_params=_params(("parallel",)),
        name="merge",
    )(x2, ya, yb, gmix, wg, wpa, wpb, wout)


def _ffn_kernel(x_ref, gffn_ref, wgate_ref, wup_ref, wdown_ref, o_ref, *, n_split):
    x = x_ref[...]
    h = _rms(x, gffn_ref[...]).astype(BF16)
    d_ff = wgate_ref.shape[1]
    w = d_ff // n_split
    acc = x
    for j in range(n_split):
        gate = _dot(h, wgate_ref[:, j * w:(j + 1) * w])
        up = _dot(h, wup_ref[:, j * w:(j + 1) * w])
        act = (gate * jax.nn.sigmoid(gate) * up).astype(BF16)
        acc = acc + _dot(act, wdown_ref[j * w:(j + 1) * w, :])
    o_ref[...] = acc


def _ffn(x2, gffn, wgate, wup, wdown, tm, n_split):
    n = x2.shape[0]
    return pl.pallas_call(
        functools.partial(_ffn_kernel, n_split=n_split),
        grid=(n // tm,),
        in_specs=[
            pl.BlockSpec((tm, D_MODEL), lambda i: (i, 0)),
            _const_spec(gffn.shape), _const_spec(wgate.shape), _const_spec(wup.shape),
            _const_spec(wdown.shape),
        ],
        out_specs=pl.BlockSpec((tm, D_MODEL), lambda i: (i, 0)),
        out_shape=jax.ShapeDtypeStruct((n, D_MODEL), F32),
        compiler_params=_params(("parallel",)),
        name="ffn",
    )(x2, gffn, wgate, wup, wdown)


def _rope_tables(seq, gain, scale):
    pos = jnp.arange(seq, dtype=F32)
    inv_freq = 1.0 / (ROPE_THETA ** (jnp.arange(0, MLA_ROPE, 2, dtype=F32) / MLA_ROPE))
    ang = pos[:, None] * inv_freq[None, :]
    cos, sin = jnp.cos(ang), jnp.sin(ang)
    half = MLA_ROPE // 2
    g_nope, g_rope = gain[:MLA_NOPE], gain[MLA_NOPE:]
    nope = jnp.broadcast_to(g_nope[None, :], (seq, MLA_NOPE))
    c_rope = jnp.concatenate([cos, cos], axis=1) * g_rope[None, :]
    g_partner = jnp.concatenate([g_rope[half:], g_rope[:half]])
    s_rope = jnp.concatenate([-sin, sin], axis=1) * g_partner[None, :]
    zeros = jnp.zeros((seq, MLA_NOPE), F32)
    q_tab = jnp.concatenate([nope, c_rope, s_rope], axis=1) * scale
    ctab = jnp.concatenate([nope, c_rope, c_rope], axis=1) * scale
    stab = jnp.concatenate([zeros, s_rope, s_rope], axis=1) * scale
    return q_tab, ctab, stab


def _layer_params(l, seq, tm_hg, w_in, mla_w_uq, mla_w_ukv, mla_q_norm, mla_k_norm, lower_bounds):
    o = [0]
    for s in IN_SPLITS:
        o.append(o[-1] + s)
    wi = w_in[l]
    half = MLA_ROPE // 2
    z64 = jnp.zeros((D_MODEL, MLA_NOPE), F32)
    w_kpe = wi[:, o[2]:o[3]]
    w_kpe_sw = jnp.concatenate([w_kpe[:, half:], w_kpe[:, :half]], axis=1)
    wmla = jnp.concatenate([wi[:, o[0]:o[2]], z64, w_kpe, w_kpe, z64, w_kpe_sw, w_kpe_sw], axis=1).astype(BF16)
    whg = wi[:, o[3]:o[7]].astype(BF16)
    wgates = wi[:, o[7]:o[9]].astype(BF16)

    uq = mla_w_uq[l].reshape(Q_LORA, MLA_HEADS, MLA_QK)
    uq_nope, uq_rope = uq[..., :MLA_NOPE], uq[..., MLA_NOPE:]
    uq_rope_sw = jnp.concatenate([uq_rope[..., half:], uq_rope[..., :half]], axis=-1)
    wuq = jnp.concatenate([uq_nope, uq_rope, uq_rope_sw], axis=-1).reshape(
        Q_LORA, MLA_HEADS * HEAD_PAD).astype(BF16)

    ukv = mla_w_ukv[l].reshape(KV_LORA, MLA_HEADS, MLA_NOPE + MLA_V)
    zk64 = jnp.zeros((KV_LORA, MLA_HEADS, HEAD_PAD - MLA_NOPE), F32)
    wuk = jnp.concatenate([ukv[..., :MLA_NOPE], zk64], axis=-1).reshape(KV_LORA, MLA_HEADS * HEAD_PAD)
    wuv = ukv[..., MLA_NOPE:].reshape(KV_LORA, MLA_HEADS * MLA_V)
    wukv = jnp.concatenate([wuk, wuv], axis=1).astype(BF16)

    tq_tab, _, _ = _rope_tables(seq, mla_q_norm[l], MLA_QK ** -0.5 * LOG2_E)
    _, ck, sk = _rope_tables(seq, mla_k_norm[l], 1.0)
    norm_mask = (jnp.arange(HEAD_PAD) < MLA_QK).astype(F32)[None, :]

    lb = lower_bounds[l][None, :]
    r = jnp.arange(tm_hg)
    tri = ((r[:, None] // HG_CHUNK == r[None, :] // HG_CHUNK) & (r[:, None] >= r[None, :])).astype(BF16)
    nc = tm_hg // HG_CHUNK
    cmask = (jnp.arange(nc * HG_DV)[:, None] // HG_DV == r[None, :] // HG_CHUNK).astype(BF16)
    return dict(wcat=jnp.concatenate([wmla, whg], axis=1), wgates=wgates, wuq=wuq, wukv=wukv,
                tabs=(tq_tab, ck, sk), norm_mask=norm_mask,
                loglb=jnp.log(lb), log1mlb=jnp.log1p(-lb), tri=tri, cmask=cmask)


def kernel(x, norm_mix, w_in, mla_norm_cq, mla_w_uq, mla_norm_ckv, mla_w_ukv, mla_q_norm, mla_k_norm,
           hg_lb_logits, hg_out_norm, w_proj_a, w_proj_b, w_out, norm_ffn, w_gate, w_up, w_down):
    batch, seq, d = x.shape
    assert d == D_MODEL
    depth = w_in.shape[0]
    tm = min(HG_SUB, seq)
    tm_hg = min(2 * HG_SUB, seq)
    tm_ffn = min(1024, seq)
    assert all(seq % t == 0 for t in (tm, tm_hg, tm_ffn))

    p = jax.nn.softmax(hg_lb_logits.astype(F32), axis=0)
    lower_bounds = jnp.maximum(jnp.cumsum(p, axis=0) - p[0:1], 0.0)

    x2 = x.reshape(batch * seq, d)
    for l in range(depth):
        lp = _layer_params(l, seq, min(HG_SUB, seq), w_in, mla_w_uq, mla_w_ukv, mla_q_norm, mla_k_norm, lower_bounds)
        gmix = norm_mix[l][None, :]
        qt, k, vt, y_b = _mixer_in(x2, gmix, lp["wcat"], mla_norm_cq[l][None, :], lp["wuq"],
                                   mla_norm_ckv[l][None, :], lp["wukv"], lp["norm_mask"], lp["tabs"],
                                   lp["loglb"], lp["log1mlb"], hg_out_norm[l][None, :], lp["tri"], lp["cmask"],
                                   batch, seq, tm_hg)
        y_a = _mla_attn(qt, k, vt, batch, seq, tm)
        x2 = _merge(x2, y_a, y_b, gmix, lp["wgates"], w_proj_a[l].astype(BF16),
                    w_proj_b[l].astype(BF16), w_out[l].astype(BF16), tm_ffn)
        x2 = _ffn(x2, norm_ffn[l][None, :], w_gate[l].astype(BF16), w_up[l].astype(BF16),
                  w_down[l].astype(BF16), tm_ffn, FFN_SPLIT)
    return x2.reshape(batch, seq, d)
```

```python
import functools

import jax
import jax.numpy as jnp
from jax import lax
from jax.experimental import pallas as pl
from jax.experimental.pallas import tpu as pltpu

D_MODEL = 1024
MLA_HEADS = 8
MLA_NOPE = 64
MLA_ROPE = 32
MLA_QK = MLA_NOPE + MLA_ROPE
MLA_V = 64
Q_LORA = 384
KV_LORA = 256
ROPE_THETA = 10000.0
HG_HEADS = 4
HG_DK = 128
HG_DV = 128
HG_KW = HG_HEADS * HG_DK
HG_WIDTH = HG_HEADS * HG_DV
NORM_EPS = 1e-6
IN_SPLITS = (Q_LORA, KV_LORA, MLA_ROPE, HG_KW, HG_KW, HG_WIDTH, HG_WIDTH, D_MODEL, D_MODEL)

LANES = 128
HEAD_PAD = LANES
ATTN_GROUP = 4
L_ROWS = 16
HG_CHUNK = 32
HG_SUB = 256
HG_SAFE_DECAY = 60.0
LOG2_E = 1.4426950408889634
MLA_IN_W = 896
MERGE_SPLIT = 2
FFN_SPLIT = 11
VMEM_LIMIT = 56 * 1024 * 1024

BF16 = jnp.bfloat16
F32 = jnp.float32
NT = (((1,), (1,)), ((), ()))
TN = (((0,), (0,)), ((), ()))


def _rms(x, g):
    return x * lax.rsqrt(jnp.mean(x * x, axis=-1, keepdims=True) + NORM_EPS) * g


def _dot(a, b):
    return jnp.dot(a, b, preferred_element_type=F32)


def _const_spec(shape):
    nd = len(shape)
    return pl.BlockSpec(shape, lambda *_: (0,) * nd, pipeline_mode=pl.Buffered(1))


def _params(sem):
    return pltpu.CompilerParams(dimension_semantics=sem, vmem_limit_bytes=VMEM_LIMIT)


def _attn_kernel(qt_ref, k_ref, vt_ref, o_ref, s_scr, m_scr, *, tq, n_q):
    step_idx = pl.program_id(2)
    kv_idx = lax.broadcasted_iota(jnp.int32, (tq, tq), 0)
    q_idx = lax.broadcasted_iota(jnp.int32, (tq, tq), 1)
    causal = q_idx >= kv_idx
    ones_rows = jnp.ones((L_ROWS, tq), BF16)

    def scores(j, iq):
        start = pl.multiple_of(j * tq, tq)
        return tuple(_dot(k_ref[pl.ds(start, tq), hh * HEAD_PAD:(hh + 1) * HEAD_PAD],
                          qt_ref[iq, hh * HEAD_PAD:(hh + 1) * HEAD_PAD, :]) for hh in range(ATTN_GROUP))

    def stash(slot, s_all):
        for hh in range(ATTN_GROUP):
            s_scr[slot, hh] = s_all[hh]
            m_scr[slot, hh] = jnp.max(s_all[hh], axis=0, keepdims=True)

    def update(j, slot, carry, masked):
        vt = vt_ref[j]
        out = []
        for hh in range(ATTN_GROUP):
            m, accl = carry[hh]
            s = s_scr[slot, hh]
            if masked:
                s = jnp.where(causal, s, -jnp.inf)
                s_max = jnp.max(s, axis=0, keepdims=True)
            else:
                s_max = m_scr[slot, hh]
            m_new = jnp.maximum(m, s_max)
            alpha = jnp.exp2(m - m_new)
            p = jnp.exp2(s - m_new).astype(BF16)
            lhs = jnp.concatenate([vt[hh * MLA_V:(hh + 1) * MLA_V, :], ones_rows], axis=0)
            out.append((m_new, alpha * accl + _dot(lhs, p)))
        return tuple(out)

    @pl.when(step_idx == 0)
    def _():
        stash(0, scores(0, 0))

    for u in range(2):
        i = 2 * step_idx + u

        def step(t, src, dst, carry, i=i):
            s_next = scores(t + 1, i)
            carry = update(t, src, carry, False)
            stash(dst, s_next)
            return carry

        def pair(jj, carry, step=step):
            return step(2 * jj + 1, 1, 0, step(2 * jj, 0, 1, carry))

        init = tuple((jnp.full((1, tq), -jnp.inf, F32), jnp.zeros((MLA_V + L_ROWS, tq), F32))
                     for _ in range(ATTN_GROUP))
        carry = lax.fori_loop(0, step_idx, pair, init)
        slot = 0
        if u == 1:
            carry = step(i - 1, 0, 1, carry)
            slot = 1
        s_ahead = scores(0, jnp.minimum(i + 1, n_q - 1))
        carry = update(i, slot, carry, True)
        stash(0, s_ahead)
        o_t = jnp.concatenate([accl[:MLA_V] / accl[MLA_V:MLA_V + 1] for _, accl in carry], axis=0)
        o_ref[u * tq:(u + 1) * tq, :] = o_t.T.astype(BF16)


def _mla_attn(qt, k, vt, batch, seq, tq):
    nq = seq // tq
    assert nq % 2 == 0
    return pl.pallas_call(
        functools.partial(_attn_kernel, tq=tq, n_q=nq),
        grid=(batch, MLA_HEADS // ATTN_GROUP, nq // 2),
        in_specs=[
            pl.BlockSpec((None, nq, ATTN_GROUP * HEAD_PAD, tq), lambda b, g, i: (b, 0, g, 0)),
            pl.BlockSpec((seq, ATTN_GROUP * HEAD_PAD), lambda b, g, i: (b, g)),
            pl.BlockSpec((None, None, nq, ATTN_GROUP * MLA_V, tq), lambda b, g, i: (b, g, 0, 0, 0)),
        ],
        out_specs=pl.BlockSpec((2 * tq, ATTN_GROUP * MLA_V), lambda b, g, i: (b * (nq // 2) + i, g)),
        out_shape=jax.ShapeDtypeStruct((batch * seq, MLA_HEADS * MLA_V), BF16),
        scratch_shapes=[
            pltpu.VMEM((2, ATTN_GROUP, tq, tq), F32),
            pltpu.VMEM((2, ATTN_GROUP, 1, tq), F32),
        ],
        compiler_params=_params(("parallel", "parallel", "arbitrary")),
        name="mla_attn",
    )(qt, k, vt)


def _split3(x):
    a = x.astype(BF16)
    r = x - a.astype(F32)
    b = r.astype(BF16)
    c = (r - b.astype(F32)).astype(BF16)
    return a, b, c


def _mixer_in_kernel(x_ref, gmix_ref, wcat_ref, gcq_ref, wuq_ref, gckv_ref, wukv_ref, nmask_ref,
                     tq_ref, ck_ref, sk_ref, loglb_ref, log1mlb_ref, gout_ref, tri_ref, cmask_ref,
                     qt_ref, k_ref, vt_ref, o_ref,
                     st_ref, stbak_ref, q_scr, k_scr, v_scr, b_scr, o_scr, g_scr, *, tm):
    S = HG_SUB
    C = HG_CHUNK
    n_sub = tm // S
    ncs = S // C
    nc = tm // C

    @pl.when(pl.program_id(1) == 0)
    def _():
        st_ref[...] = jnp.zeros_like(st_ref)

    stbak_ref[...] = st_ref[...]
    gmix = gmix_ref[...]
    zzs = [_dot(_rms(x_ref[u * S:(u + 1) * S, :], gmix).astype(BF16), wcat_ref[...]) for u in range(n_sub)]
    zs = [zz[:, MLA_IN_W:] for zz in zzs]
    nmask = nmask_ref[...]
    inv_qk = 1.0 / MLA_QK
    gw = ATTN_GROUP * MLA_V

    tri = tri_ref[...]
    cmask = cmask_ref[...]
    row = lax.broadcasted_iota(jnp.int32, (S, S), 0)
    col = lax.broadcasted_iota(jnp.int32, (S, S), 1)
    causal = (row >= col) & (row // C == col // C)
    a = loglb_ref[...]
    log1mlb = log1mlb_ref[...]
    sts = [st_ref[hd] for hd in range(HG_HEADS)]
    b_mins = []
    for u in range(n_sub):
        rows = slice(u * S, (u + 1) * S)
        zc = zzs[u][:, :MLA_IN_W]
        c_q = zc[:, :Q_LORA]
        c_kv = zc[:, Q_LORA:Q_LORA + KV_LORA]
        kpe_a = zc[:, 640:768]
        kpe_b = zc[:, 768:896]
        q2 = _dot(_rms(c_q, gcq_ref[...]).astype(BF16), wuq_ref[...])
        kv2 = _dot(_rms(c_kv, gckv_ref[...]).astype(BF16), wukv_ref[...])
        tq_tab, ck, sk = tq_ref[rows, :], ck_ref[rows, :], sk_ref[rows, :]
        for hd in range(MLA_HEADS):
            lo = hd * HEAD_PAD
            qa = q2[:, lo:lo + HEAD_PAD]
            rq = lax.rsqrt(jnp.sum(qa * qa * nmask, axis=-1, keepdims=True) * inv_qk + NORM_EPS)
            qt_ref[u, lo:lo + HEAD_PAD, :] = (qa * tq_tab * rq).astype(BF16).T
            ka = kv2[:, lo:lo + HEAD_PAD] + kpe_a
            rk = lax.rsqrt(jnp.sum(ka * ka * nmask, axis=-1, keepdims=True) * inv_qk + NORM_EPS)
            k_ref[rows, lo:lo + HEAD_PAD] = ((ka * ck + kpe_b * sk) * rk).astype(BF16)
        v = kv2[:, MLA_HEADS * HEAD_PAD:]
        for g in range(MLA_HEADS // ATTN_GROUP):
            vt_ref[g, u] = v[:, g * gw:(g + 1) * gw].astype(BF16).T

        z = zs[u]
        zq = z[:, :HG_KW]
        zf = z[:, HG_KW:2 * HG_KW]
        zi = z[:, 2 * HG_KW:2 * HG_KW + HG_WIDTH]
        zg = z[:, 2 * HG_KW + HG_WIDTH:]
        log_sig = jnp.minimum(zf, 0.0) - jnp.log(1.0 + jnp.exp(-jnp.abs(zf)))
        c = log1mlb + log_sig
        log_f = jnp.maximum(a, c) + jnp.log(1.0 + jnp.exp(-jnp.abs(a - c)))
        kk = 1.0 - jnp.exp(log_f)
        l1, l2, l3 = _split3(log_f)
        b = _dot(tri, l1) + _dot(tri, l2) + _dot(tri, l3)
        b3 = b.reshape(ncs, C, HG_KW)
        b_last = b3[:, C - 1:C, :]
        q_scr[rows, :] = zq
        k_scr[rows, :] = kk
        v_scr[rows, :] = zi
        b_scr[rows, :] = b
        g_scr[rows, :] = zg * jax.nn.sigmoid(zg)
        b_mins.append(jnp.min(b))

        qe = (zq * jnp.exp(b)).astype(BF16)
        ke = (kk * jnp.exp(jnp.minimum(-b, HG_SAFE_DECAY + 20.0))).astype(BF16)
        kl = (kk.reshape(ncs, C, HG_KW) * jnp.exp(b_last - b3)).reshape(S, HG_KW).astype(BF16)
        e_last = jnp.exp(b_last)
        vb = zi.astype(BF16)
        vt = vb.T
        for hd in range(HG_HEADS):
            lo = hd * HG_DK
            qe_h = qe[:, lo:lo + HG_DK]
            att = lax.dot_general(qe_h, ke[:, lo:lo + HG_DK], NT, preferred_element_type=F32)
            att = jnp.where(causal, att, 0.0).astype(BF16)
            o_intra = _dot(att, vb[:, lo:lo + HG_DV])
            vt_h = vt[lo:lo + HG_DV, :]
            upd = _dot(jnp.concatenate([vt_h] * ncs, axis=0) * cmask, kl[:, lo:lo + HG_DK])
            st = sts[hd]
            for ci in range(ncs):
                r0 = ci * C
                o_inter = lax.dot_general(qe_h[r0:r0 + C], st.astype(BF16), NT, preferred_element_type=F32)
                o_scr[u * S + r0:u * S + r0 + C, lo:lo + HG_DV] = o_intra[r0:r0 + C] + o_inter
                st = st * e_last[ci, :, lo:lo + HG_DK] + upd[ci * HG_DV:(ci + 1) * HG_DV]
            sts[hd] = st
    for hd in range(HG_HEADS):
        st_ref[hd] = sts[hd]
    b_min = b_mins[0]
    for u in range(1, n_sub):
        b_min = jnp.minimum(b_min, b_mins[u])
    unsafe = b_min < -HG_SAFE_DECAY

    @pl.when(unsafe)
    def _():
        st_ref[...] = stbak_ref[...]
        t_idx = lax.broadcasted_iota(jnp.int32, (C, 1), 0)
        for hd in range(HG_HEADS):
            lo = hd * HG_DK

            def chunk_body(ci, _, lo=lo, hd=hd):
                r0 = pl.multiple_of(ci * C, C)
                q_c = q_scr[pl.ds(r0, C), lo:lo + HG_DK]
                k_c = k_scr[pl.ds(r0, C), lo:lo + HG_DK]
                v_c = v_scr[pl.ds(r0, C), lo:lo + HG_DV]
                b_c = b_scr[pl.ds(r0, C), lo:lo + HG_DK]
                bl = b_c[C - 1:C, :]

                def s_body(s, o_acc):
                    pick = t_idx == s
                    ks = jnp.sum(jnp.where(pick, k_c, 0.0), axis=0, keepdims=True)
                    bs = jnp.sum(jnp.where(pick, b_c, 0.0), axis=0, keepdims=True)
                    vs = jnp.sum(jnp.where(pick, v_c, 0.0), axis=0, keepdims=True)
                    dec = jnp.exp(jnp.minimum(b_c - bs, 0.0))
                    w = jnp.sum(q_c * ks * dec, axis=-1, keepdims=True)
                    w = jnp.where(t_idx >= s, w, 0.0)
                    return o_acc + w * vs

                o_intra = lax.fori_loop(0, C, s_body, jnp.zeros((C, HG_DV), F32))
                st = st_ref[hd]
                qe_c = (q_c * jnp.exp(b_c)).astype(BF16)
                o_inter = lax.dot_general(qe_c, st.astype(BF16), NT, preferred_element_type=F32)
                o_scr[pl.ds(r0, C), lo:lo + HG_DV] = o_intra + o_inter
                kl_c = (k_c * jnp.exp(bl - b_c)).astype(BF16)
                upd = lax.dot_general(v_c.astype(BF16), kl_c, TN, preferred_element_type=F32)
                st_ref[hd] = st * jnp.exp(bl) + upd
                return 0

            lax.fori_loop(0, nc, chunk_body, 0)

    o = o_scr[...]
    gate = g_scr[...]
    gout = gout_ref[...]
    for hd in range(HG_HEADS):
        lo = hd * HG_DV
        o_ref[:, lo:lo + HG_DV] = (_rms(o[:, lo:lo + HG_DV], gout) * gate[:, lo:lo + HG_DV]).astype(BF16)


def _mixer_in(x2, gmix, wcat, gcq, wuq, gckv, wukv, nmask, tabs, loglb, log1mlb, gout, tri, cmask,
              batch, seq, tm):
    nt = seq // tm
    n_sub = tm // HG_SUB
    n = batch * seq
    tab_spec = pl.BlockSpec((tm, LANES), lambda b, i: (i, 0))

    def row_spec(width):
        return pl.BlockSpec((tm, width), lambda b, i: (b * nt + i, 0))

    return pl.pallas_call(
        functools.partial(_mixer_in_kernel, tm=tm),
        grid=(batch, nt),
        in_specs=[
            row_spec(D_MODEL),
            _const_spec(gmix.shape), _const_spec(wcat.shape), _const_spec(gcq.shape),
            _const_spec(wuq.shape), _const_spec(gckv.shape), _const_spec(wukv.shape),
            _const_spec(nmask.shape), tab_spec, tab_spec, tab_spec,
            _const_spec(loglb.shape), _const_spec(log1mlb.shape), _const_spec(gout.shape),
            _const_spec(tri.shape), _const_spec(cmask.shape),
        ],
        out_specs=[
            pl.BlockSpec((None, n_sub, MLA_HEADS * HEAD_PAD, HG_SUB), lambda b, i: (b, i, 0, 0)),
            row_spec(MLA_HEADS * HEAD_PAD),
            pl.BlockSpec((None, MLA_HEADS // ATTN_GROUP, n_sub, ATTN_GROUP * MLA_V, HG_SUB),
                         lambda b, i: (b, 0, i, 0, 0)),
            row_spec(HG_WIDTH),
        ],
        out_shape=[
            jax.ShapeDtypeStruct((batch, seq // HG_SUB, MLA_HEADS * HEAD_PAD, HG_SUB), BF16),
            jax.ShapeDtypeStruct((n, MLA_HEADS * HEAD_PAD), BF16),
            jax.ShapeDtypeStruct((batch, MLA_HEADS // ATTN_GROUP, seq // HG_SUB, ATTN_GROUP * MLA_V, HG_SUB), BF16),
            jax.ShapeDtypeStruct((n, HG_WIDTH), BF16),
        ],
        scratch_shapes=[
            pltpu.VMEM((HG_HEADS, HG_DV, HG_DK), F32),
            pltpu.VMEM((HG_HEADS, HG_DV, HG_DK), F32),
            pltpu.VMEM((tm, HG_KW), F32),
            pltpu.VMEM((tm, HG_KW), F32),
            pltpu.VMEM((tm, HG_WIDTH), F32),
            pltpu.VMEM((tm, HG_KW), F32),
            pltpu.VMEM((tm, HG_WIDTH), F32),
            pltpu.VMEM((tm, HG_WIDTH), F32),
        ],
        compiler_params=_params(("parallel", "arbitrary")),
        name="mixer_in",
    )(x2, gmix, wcat, gcq, wuq, gckv, wukv, nmask, *tabs, loglb, log1mlb, gout, tri, cmask)


def _merge_kernel(x_ref, ya_ref, yb_ref, gmix_ref, wg_ref, wpa_ref, wpb_ref, wout_ref, o_ref):
    x = x_ref[...]
    h = _rms(x, gmix_ref[...]).astype(BF16)
    ya, yb = ya_ref[...], yb_ref[...]
    acc = x
    w = D_MODEL // MERGE_SPLIT
    for j in range(MERGE_SPLIT):
        cols = slice(j * w, (j + 1) * w)
        g_a = _dot(h, wg_ref[:, cols])
        g_b = _dot(h, wg_ref[:, D_MODEL + j * w:D_MODEL + (j + 1) * w])
        y = (jax.nn.sigmoid(g_a) * _dot(ya, wpa_ref[:, cols])
             + jax.nn.sigmoid(g_b) * _dot(yb, wpb_ref[:, cols]))
        acc = acc + _dot(y.astype(BF16), wout_ref[cols, :])
    o_ref[...] = acc


def _merge(x2, ya, yb, gmix, wg, wpa, wpb, wout, tm):
    n = x2.shape[0]
    return pl.pallas_call(
        _merge_kernel,
        grid=(n // tm,),
        in_specs=[
            pl.BlockSpec((tm, D_MODEL), lambda i: (i, 0)),
            pl.BlockSpec((tm, ya.shape[1]), lambda i: (i, 0)),
            pl.BlockSpec((tm, yb.shape[1]), lambda i: (i, 0)),
            _const_spec(gmix.shape), _const_spec(wg.shape), _const_spec(wpa.shape),
            _const_spec(wpb.shape), _const_spec(wout.shape),
        ],
        out_specs=pl.BlockSpec((tm, D_MODEL), lambda i: (i, 0)),
        out_shape=jax.ShapeDtypeStruct((n, D_MODEL), F32),
        compiler_params=_params(("parallel",)),
        name="merge",
    )(x2, ya, yb, gmix, wg, wpa, wpb, wout)


def _ffn_kernel(x_ref, gffn_ref, wgate_ref, wup_ref, wdown_ref, o_ref, *, n_split):
    x = x_ref[...]
    h = _rms(x, gffn_ref[...]).astype(BF16)
    d_ff = wgate_ref.shape[1]
    w = d_ff // n_split
    acc = x
    for j in range(n_split):
        gate = _dot(h, wgate_ref[:, j * w:(j + 1) * w])
        up = _dot(h, wup_ref[:, j * w:(j + 1) * w])
        act = (gate * jax.nn.sigmoid(gate) * up).astype(BF16)
        acc = acc + _dot(act, wdown_ref[j * w:(j + 1) * w, :])
    o_ref[...] = acc


def _ffn(x2, gffn, wgate, wup, wdown, tm, n_split):
    n = x2.shape[0]
    return pl.pallas_call(
        functools.partial(_ffn_kernel, n_split=n_split),
        grid=(n // tm,),
        in_specs=[
            pl.BlockSpec((tm, D_MODEL), lambda i: (i, 0)),
            _const_spec(gffn.shape), _const_spec(wgate.shape), _const_spec(wup.shape),
            _const_spec(wdown.shape),
        ],
        out_specs=pl.BlockSpec((tm, D_MODEL), lambda i: (i, 0)),
        out_shape=jax.ShapeDtypeStruct((n, D_MODEL), F32),
        compiler_params=_params(("parallel",)),
        name="ffn",
    )(x2, gffn, wgate, wup, wdown)


def _rope_tables(seq, gain, scale):
    pos = jnp.arange(seq, dtype=F32)
    inv_freq = 1.0 / (ROPE_THETA ** (jnp.arange(0, MLA_ROPE, 2, dtype=F32) / MLA_ROPE))
    ang = pos[:, None] * inv_freq[None, :]
    cos, sin = jnp.cos(ang), jnp.sin(ang)
    half = MLA_ROPE // 2
    g_nope, g_rope = gain[:MLA_NOPE], gain[MLA_NOPE:]
    nope = jnp.broadcast_to(g_nope[None, :], (seq, MLA_NOPE))
    c_rope = jnp.concatenate([cos, cos], axis=1) * g_rope[None, :]
    g_partner = jnp.concatenate([g_rope[half:], g_rope[:half]])
    s_rope = jnp.concatenate([-sin, sin], axis=1) * g_partner[None, :]
    zeros = jnp.zeros((seq, MLA_NOPE), F32)
    q_tab = jnp.concatenate([nope, c_rope, s_rope], axis=1) * scale
    ctab = jnp.concatenate([nope, c_rope, c_rope], axis=1) * scale
    stab = jnp.concatenate([zeros, s_rope, s_rope], axis=1) * scale
    return q_tab, ctab, stab


def _layer_params(l, seq, tm_hg, w_in, mla_w_uq, mla_w_ukv, mla_q_norm, mla_k_norm, lower_bounds):
    o = [0]
    for s in IN_SPLITS:
        o.append(o[-1] + s)
    wi = w_in[l]
    half = MLA_ROPE // 2
    z64 = jnp.zeros((D_MODEL, MLA_NOPE), F32)
    w_kpe = wi[:, o[2]:o[3]]
    w_kpe_sw = jnp.concatenate([w_kpe[:, half:], w_kpe[:, :half]], axis=1)
    wmla = jnp.concatenate([wi[:, o[0]:o[2]], z64, w_kpe, w_kpe, z64, w_kpe_sw, w_kpe_sw], axis=1).astype(BF16)
    whg = wi[:, o[3]:o[7]].astype(BF16)
    wgates = wi[:, o[7]:o[9]].astype(BF16)

    uq = mla_w_uq[l].reshape(Q_LORA, MLA_HEADS, MLA_QK)
    uq_nope, uq_rope = uq[..., :MLA_NOPE], uq[..., MLA_NOPE:]
    uq_rope_sw = jnp.concatenate([uq_rope[..., half:], uq_rope[..., :half]], axis=-1)
    wuq = jnp.concatenate([uq_nope, uq_rope, uq_rope_sw], axis=-1).reshape(
        Q_LORA, MLA_HEADS * HEAD_PAD).astype(BF16)

    ukv = mla_w_ukv[l].reshape(KV_LORA, MLA_HEADS, MLA_NOPE + MLA_V)
    zk64 = jnp.zeros((KV_LORA, MLA_HEADS, HEAD_PAD - MLA_NOPE), F32)
    wuk = jnp.concatenate([ukv[..., :MLA_NOPE], zk64], axis=-1).reshape(KV_LORA, MLA_HEADS * HEAD_PAD)
    wuv = ukv[..., MLA_NOPE:].reshape(KV_LORA, MLA_HEADS * MLA_V)
    wukv = jnp.concatenate([wuk, wuv], axis=1).astype(BF16)

    tq_tab, _, _ = _rope_tables(seq, mla_q_norm[l], MLA_QK ** -0.5 * LOG2_E)
    _, ck, sk = _rope_tables(seq, mla_k_norm[l], 1.0)
    norm_mask = (jnp.arange(HEAD_PAD) < MLA_QK).astype(F32)[None, :]

    lb = lower_bounds[l][None, :]
    r = jnp.arange(tm_hg)
    tri = ((r[:, None] // HG_CHUNK == r[None, :] // HG_CHUNK) & (r[:, None] >= r[None, :])).astype(BF16)
    nc = tm_hg // HG_CHUNK
    cmask = (jnp.arange(nc * HG_DV)[:, None] // HG_DV == r[None, :] // HG_CHUNK).astype(BF16)
    return dict(wcat=jnp.concatenate([wmla, whg], axis=1), wgates=wgates, wuq=wuq, wukv=wukv,
                tabs=(tq_tab, ck, sk), norm_mask=norm_mask,
                loglb=jnp.log(lb), log1mlb=jnp.log1p(-lb), tri=tri, cmask=cmask)


def kernel(x, norm_mix, w_in, mla_norm_cq, mla_w_uq, mla_norm_ckv, mla_w_ukv, mla_q_norm, mla_k_norm,
           hg_lb_logits, hg_out_norm, w_proj_a, w_proj_b, w_out, norm_ffn, w_gate, w_up, w_down):
    batch, seq, d = x.shape
    assert d == D_MODEL
    depth = w_in.shape[0]
    tm = min(HG_SUB, seq)
    tm_hg = min(2 * HG_SUB, seq)
    tm_ffn = min(1024, seq)
    assert all(seq % t == 0 for t in (tm, tm_hg, tm_ffn))

    p = jax.nn.softmax(hg_lb_logits.astype(F32), axis=0)
    lower_bounds = jnp.maximum(jnp.cumsum(p, axis=0) - p[0:1], 0.0)

    x2 = x.reshape(batch * seq, d)
    for l in range(depth):
        lp = _layer_params(l, seq, min(HG_SUB, seq), w_in, mla_w_uq, mla_w_ukv, mla_q_norm, mla_k_norm, lower_bounds)
        gmix = norm_mix[l][None, :]
        qt, k, vt, y_b = _mixer_in(x2, gmix, lp["wcat"], mla_norm_cq[l][None, :], lp["wuq"],
                                   mla_norm_ckv[l][None, :], lp["wukv"], lp["norm_mask"], lp["tabs"],
                                   lp["loglb"], lp["log1mlb"], hg_out_norm[l][None, :], lp["tri"], lp["cmask"],
                                   batch, seq, tm_hg)
        y_a = _mla_attn(qt, k, vt, batch, seq, tm)
        x2 = _merge(x2, y_a, y_b, gmix, lp["wgates"], w_proj_a[l].astype(BF16),
                    w_proj_b[l].astype(BF16), w_out[l].astype(BF16), tm_ffn)
        x2 = _ffn(x2, norm_ffn[l][None, :], w_gate[l].astype(BF16), w_up[l].astype(BF16),
                  w_down[l].astype(BF16), tm_ffn, FFN_SPLIT)
    return x2.reshape(batch, seq, d)
```

```python
import functools

import jax
import jax.numpy as jnp
from jax import lax
from jax.experimental import pallas as pl
from jax.experimental.pallas import tpu as pltpu

D_MODEL = 1024
MLA_HEADS = 8
MLA_NOPE = 64
MLA_ROPE = 32
MLA_QK = MLA_NOPE + MLA_ROPE
MLA_V = 64
Q_LORA = 384
KV_LORA = 256
ROPE_THETA = 10000.0
HG_HEADS = 4
HG_DK = 128
HG_DV = 128
HG_KW = HG_HEADS * HG_DK
HG_WIDTH = HG_HEADS * HG_DV
NORM_EPS = 1e-6
IN_SPLITS = (Q_LORA, KV_LORA, MLA_ROPE, HG_KW, HG_KW, HG_WIDTH, HG_WIDTH, D_MODEL, D_MODEL)

LANES = 128
HEAD_PAD = LANES
ATTN_GROUP = 4
L_ROWS = 16
HG_CHUNK = 32
HG_SUB = 256
HG_SAFE_DECAY = 60.0
LOG2_E = 1.4426950408889634
MLA_IN_W = 896
MERGE_SPLIT = 2
FFN_SPLIT = 11
VMEM_LIMIT = 56 * 1024 * 1024

BF16 = jnp.bfloat16
F32 = jnp.float32
NT = (((1,), (1,)), ((), ()))
TN = (((0,), (0,)), ((), ()))


def _rms(x, g):
    return x * lax.rsqrt(jnp.mean(x * x, axis=-1, keepdims=True) + NORM_EPS) * g


def _dot(a, b):
    return jnp.dot(a, b, preferred_element_type=F32)


def _const_spec(shape):
    nd = len(shape)
    return pl.BlockSpec(shape, lambda *_: (0,) * nd, pipeline_mode=pl.Buffered(1))


def _params(sem):
    return pltpu.CompilerParams(dimension_semantics=sem, vmem_limit_bytes=VMEM_LIMIT)


def _attn_kernel(qt_ref, k_ref, vt_ref, o_ref, s_scr, m_scr, *, tq, n_q):
    step_idx = pl.program_id(2)
    kv_idx = lax.broadcasted_iota(jnp.int32, (tq, tq), 0)
    q_idx = lax.broadcasted_iota(jnp.int32, (tq, tq), 1)
    causal = q_idx >= kv_idx
    ones_rows = jnp.ones((L_ROWS, tq), BF16)

    def scores(j, iq):
        start = pl.multiple_of(j * tq, tq)
        return tuple(_dot(k_ref[pl.ds(start, tq), hh * HEAD_PAD:(hh + 1) * HEAD_PAD],
                          qt_ref[iq, hh * HEAD_PAD:(hh + 1) * HEAD_PAD, :]) for hh in range(ATTN_GROUP))

    def stash(slot, s_all):
        for hh in range(ATTN_GROUP):
            s_scr[slot, hh] = s_all[hh]
            m_scr[slot, hh] = jnp.max(s_all[hh], axis=0, keepdims=True)

    def update(j, slot, carry, masked):
        vt = vt_ref[j]
        out = []
        for hh in range(ATTN_GROUP):
            m, accl = carry[hh]
            s = s_scr[slot, hh]
            if masked:
                s = jnp.where(causal, s, -jnp.inf)
                s_max = jnp.max(s, axis=0, keepdims=True)
            else:
                s_max = m_scr[slot, hh]
            m_new = jnp.maximum(m, s_max)
            alpha = jnp.exp2(m - m_new)
            p = jnp.exp2(s - m_new).astype(BF16)
            lhs = jnp.concatenate([vt[hh * MLA_V:(hh + 1) * MLA_V, :], ones_rows], axis=0)
            out.append((m_new, alpha * accl + _dot(lhs, p)))
        return tuple(out)

    @pl.when(step_idx == 0)
    def _():
        stash(0, scores(0, 0))

    for u in range(2):
        i = 2 * step_idx + u

        def step(t, src, dst, carry, i=i):
            s_next = scores(t + 1, i)
            carry = update(t, src, carry, False)
            stash(dst, s_next)
            return carry

        def pair(jj, carry, step=step):
            return step(2 * jj + 1, 1, 0, step(2 * jj, 0, 1, carry))

        init = tuple((jnp.full((1, tq), -jnp.inf, F32), jnp.zeros((MLA_V + L_ROWS, tq), F32))
                     for _ in range(ATTN_GROUP))
        carry = lax.fori_loop(0, step_idx, pair, init)
        slot = 0
        if u == 1:
            carry = step(i - 1, 0, 1, carry)
            slot = 1
        s_ahead = scores(0, jnp.minimum(i + 1, n_q - 1))
        carry = update(i, slot, carry, True)
        stash(0, s_ahead)
        o_t = jnp.concatenate([accl[:MLA_V] / accl[MLA_V:MLA_V + 1] for _, accl in carry], axis=0)
        o_ref[u * tq:(u + 1) * tq, :] = o_t.T.astype(BF16)


def _mla_attn(qt, k, vt, batch, seq, tq):
    nq = seq // tq
    assert nq % 2 == 0
    return pl.pallas_call(
        functools.partial(_attn_kernel, tq=tq, n_q=nq),
        grid=(batch, MLA_HEADS // ATTN_GROUP, nq // 2),
        in_specs=[
            pl.BlockSpec((None, nq, ATTN_GROUP * HEAD_PAD, tq), lambda b, g, i: (b, 0, g, 0)),
            pl.BlockSpec((None, None, seq, ATTN_GROUP * HEAD_PAD), lambda b, g, i: (b, g, 0, 0)),
            pl.BlockSpec((None, None, nq, ATTN_GROUP * MLA_V, tq), lambda b, g, i: (b, g, 0, 0, 0)),
        ],
        out_specs=pl.BlockSpec((2 * tq, ATTN_GROUP * MLA_V), lambda b, g, i: (b * (nq // 2) + i, g)),
        out_shape=jax.ShapeDtypeStruct((batch * seq, MLA_HEADS * MLA_V), BF16),
        scratch_shapes=[
            pltpu.VMEM((2, ATTN_GROUP, tq, tq), F32),
            pltpu.VMEM((2, ATTN_GROUP, 1, tq), F32),
        ],
        compiler_params=_params(("parallel", "parallel", "arbitrary")),
        name="mla_attn",
    )(qt, k, vt)


def _split3(x):
    a = x.astype(BF16)
    r = x - a.astype(F32)
    b = r.astype(BF16)
    c = (r - b.astype(F32)).astype(BF16)
    return a, b, c


def _mixer_in_kernel(x_ref, gmix_ref, wcat_ref, gcq_ref, wuq_ref, gckv_ref, wukv_ref, nmask_ref,
                     tq_ref, ck_ref, sk_ref, loglb_ref, log1mlb_ref, gout_ref, tri_ref, cmask_ref,
                     qt_ref, k_ref, vt_ref, o_ref,
                     st_ref, stbak_ref, q_scr, k_scr, v_scr, b_scr, o_scr, g_scr, *, tm):
    S = HG_SUB
    C = HG_CHUNK
    n_sub = tm // S
    ncs = S // C
    nc = tm // C

    @pl.when(pl.program_id(1) == 0)
    def _():
        st_ref[...] = jnp.zeros_like(st_ref)

    stbak_ref[...] = st_ref[...]
    gmix = gmix_ref[...]
    zzs = [_dot(_rms(x_ref[u * S:(u + 1) * S, :], gmix).astype(BF16), wcat_ref[...]) for u in range(n_sub)]
    zs = [zz[:, MLA_IN_W:] for zz in zzs]
    nmask = nmask_ref[...]
    inv_qk = 1.0 / MLA_QK
    gw = ATTN_GROUP * MLA_V

    tri = tri_ref[...]
    cmask = cmask_ref[...]
    row = lax.broadcasted_iota(jnp.int32, (S, S), 0)
    col = lax.broadcasted_iota(jnp.int32, (S, S), 1)
    causal = (row >= col) & (row // C == col // C)
    a = loglb_ref[...]
    log1mlb = log1mlb_ref[...]
    sts = [st_ref[hd] for hd in range(HG_HEADS)]
    b_mins = []
    for u in range(n_sub):
        rows = slice(u * S, (u + 1) * S)
        zc = zzs[u][:, :MLA_IN_W]
        c_q = zc[:, :Q_LORA]
        c_kv = zc[:, Q_LORA:Q_LORA + KV_LORA]
        kpe_a = zc[:, 640:768]
        kpe_b = zc[:, 768:896]
        q2 = _dot(_rms(c_q, gcq_ref[...]).astype(BF16), wuq_ref[...])
        kv2 = _dot(_rms(c_kv, gckv_ref[...]).astype(BF16), wukv_ref[...])
        tq_tab, ck, sk = tq_ref[rows, :], ck_ref[rows, :], sk_ref[rows, :]
        for hd in range(MLA_HEADS):
            lo = hd * HEAD_PAD
            qa = q2[:, lo:lo + HEAD_PAD]
            rq = lax.rsqrt(jnp.sum(qa * qa * nmask, axis=-1, keepdims=True) * inv_qk + NORM_EPS)
            qt_ref[u, lo:lo + HEAD_PAD, :] = (qa * tq_tab * rq).astype(BF16).T
            ka = kv2[:, lo:lo + HEAD_PAD] + kpe_a
            rk = lax.rsqrt(jnp.sum(ka * ka * nmask, axis=-1, keepdims=True) * inv_qk + NORM_EPS)
            k_ref[hd // ATTN_GROUP, rows, (hd % ATTN_GROUP) * HEAD_PAD:(hd % ATTN_GROUP + 1) * HEAD_PAD] = (
                (ka * ck + kpe_b * sk) * rk).astype(BF16)
        v = kv2[:, MLA_HEADS * HEAD_PAD:]
        for g in range(MLA_HEADS // ATTN_GROUP):
            vt_ref[g, u] = v[:, g * gw:(g + 1) * gw].astype(BF16).T

        z = zs[u]
        zq = z[:, :HG_KW]
        zf = z[:, HG_KW:2 * HG_KW]
        zi = z[:, 2 * HG_KW:2 * HG_KW + HG_WIDTH]
        zg = z[:, 2 * HG_KW + HG_WIDTH:]
        log_sig = jnp.minimum(zf, 0.0) - jnp.log(1.0 + jnp.exp(-jnp.abs(zf)))
        c = log1mlb + log_sig
        log_f = jnp.maximum(a, c) + jnp.log(1.0 + jnp.exp(-jnp.abs(a - c)))
        kk = 1.0 - jnp.exp(log_f)
        l1, l2, l3 = _split3(log_f)
        b = _dot(tri, l1) + _dot(tri, l2) + _dot(tri, l3)
        b3 = b.reshape(ncs, C, HG_KW)
        b_last = b3[:, C - 1:C, :]
        q_scr[rows, :] = zq
        k_scr[rows, :] = kk
        v_scr[rows, :] = zi
        b_scr[rows, :] = b
        g_scr[rows, :] = zg * jax.nn.sigmoid(zg)
        b_mins.append(jnp.min(b))

        qe = (zq * jnp.exp(b)).astype(BF16)
        ke = (kk * jnp.exp(jnp.minimum(-b, HG_SAFE_DECAY + 20.0))).astype(BF16)
        kl = (kk.reshape(ncs, C, HG_KW) * jnp.exp(b_last - b3)).reshape(S, HG_KW).astype(BF16)
        e_last = jnp.exp(b_last)
        vb = zi.astype(BF16)
        vt = vb.T
        for hd in range(HG_HEADS):
            lo = hd * HG_DK
            qe_h = qe[:, lo:lo + HG_DK]
            att = lax.dot_general(qe_h, ke[:, lo:lo + HG_DK], NT, preferred_element_type=F32)
            att = jnp.where(causal, att, 0.0).astype(BF16)
            o_intra = _dot(att, vb[:, lo:lo + HG_DV])
            vt_h = vt[lo:lo + HG_DV, :]
            upd = _dot(jnp.concatenate([vt_h] * ncs, axis=0) * cmask, kl[:, lo:lo + HG_DK])
            st = sts[hd]
            for ci in range(ncs):
                r0 = ci * C
                o_inter = lax.dot_general(qe_h[r0:r0 + C], st.astype(BF16), NT, preferred_element_type=F32)
                o_scr[u * S + r0:u * S + r0 + C, lo:lo + HG_DV] = o_intra[r0:r0 + C] + o_inter
                st = st * e_last[ci, :, lo:lo + HG_DK] + upd[ci * HG_DV:(ci + 1) * HG_DV]
            sts[hd] = st
    for hd in range(HG_HEADS):
        st_ref[hd] = sts[hd]
    b_min = b_mins[0]
    for u in range(1, n_sub):
        b_min = jnp.minimum(b_min, b_mins[u])
    unsafe = b_min < -HG_SAFE_DECAY

    @pl.when(unsafe)
    def _():
        st_ref[...] = stbak_ref[...]
        t_idx = lax.broadcasted_iota(jnp.int32, (C, 1), 0)
        for hd in range(HG_HEADS):
            lo = hd * HG_DK

            def chunk_body(ci, _, lo=lo, hd=hd):
                r0 = pl.multiple_of(ci * C, C)
                q_c = q_scr[pl.ds(r0, C), lo:lo + HG_DK]
                k_c = k_scr[pl.ds(r0, C), lo:lo + HG_DK]
                v_c = v_scr[pl.ds(r0, C), lo:lo + HG_DV]
                b_c = b_scr[pl.ds(r0, C), lo:lo + HG_DK]
                bl = b_c[C - 1:C, :]

                def s_body(s, o_acc):
                    pick = t_idx == s
                    ks = jnp.sum(jnp.where(pick, k_c, 0.0), axis=0, keepdims=True)
                    bs = jnp.sum(jnp.where(pick, b_c, 0.0), axis=0, keepdims=True)
                    vs = jnp.sum(jnp.where(pick, v_c, 0.0), axis=0, keepdims=True)
                    dec = jnp.exp(jnp.minimum(b_c - bs, 0.0))
                    w = jnp.sum(q_c * ks * dec, axis=-1, keepdims=True)
                    w = jnp.where(t_idx >= s, w, 0.0)
                    return o_acc + w * vs

                o_intra = lax.fori_loop(0, C, s_body, jnp.zeros((C, HG_DV), F32))
                st = st_ref[hd]
                qe_c = (q_c * jnp.exp(b_c)).astype(BF16)
                o_inter = lax.dot_general(qe_c, st.astype(BF16), NT, preferred_element_type=F32)
                o_scr[pl.ds(r0, C), lo:lo + HG_DV] = o_intra + o_inter
                kl_c = (k_c * jnp.exp(bl - b_c)).astype(BF16)
                upd = lax.dot_general(v_c.astype(BF16), kl_c, TN, preferred_element_type=F32)
                st_ref[hd] = st * jnp.exp(bl) + upd
                return 0

            lax.fori_loop(0, nc, chunk_body, 0)

    o = o_scr[...]
    gate = g_scr[...]
    gout = gout_ref[...]
    for hd in range(HG_HEADS):
        lo = hd * HG_DV
        o_ref[:, lo:lo + HG_DV] = (_rms(o[:, lo:lo + HG_DV], gout) * gate[:, lo:lo + HG_DV]).astype(BF16)


def _mixer_in(x2, gmix, wcat, gcq, wuq, gckv, wukv, nmask, tabs, loglb, log1mlb, gout, tri, cmask,
              batch, seq, tm):
    nt = seq // tm
    n_sub = tm // HG_SUB
    n = batch * seq
    tab_spec = pl.BlockSpec((tm, LANES), lambda b, i: (i, 0))

    def row_spec(width):
        return pl.BlockSpec((tm, width), lambda b, i: (b * nt + i, 0))

    return pl.pallas_call(
        functools.partial(_mixer_in_kernel, tm=tm),
        grid=(batch, nt),
        in_specs=[
            row_spec(D_MODEL),
            _const_spec(gmix.shape), _const_spec(wcat.shape), _const_spec(gcq.shape),
            _const_spec(wuq.shape), _const_spec(gckv.shape), _const_spec(wukv.shape),
            _const_spec(nmask.shape), tab_spec, tab_spec, tab_spec,
            _const_spec(loglb.shape), _const_spec(log1mlb.shape), _const_spec(gout.shape),
            _const_spec(tri.shape), _const_spec(cmask.shape),
        ],
        out_specs=[
            pl.BlockSpec((None, n_sub, MLA_HEADS * HEAD_PAD, HG_SUB), lambda b, i: (b, i, 0, 0)),
            pl.BlockSpec((None, MLA_HEADS // ATTN_GROUP, tm, ATTN_GROUP * HEAD_PAD), lambda b, i: (b, 0, i, 0)),
            pl.BlockSpec((None, MLA_HEADS // ATTN_GROUP, n_sub, ATTN_GROUP * MLA_V, HG_SUB),
                         lambda b, i: (b, 0, i, 0, 0)),
            row_spec(HG_WIDTH),
        ],
        out_shape=[
            jax.ShapeDtypeStruct((batch, seq // HG_SUB, MLA_HEADS * HEAD_PAD, HG_SUB), BF16),
            jax.ShapeDtypeStruct((batch, MLA_HEADS // ATTN_GROUP, seq, ATTN_GROUP * HEAD_PAD), BF16),
            jax.ShapeDtypeStruct((batch, MLA_HEADS // ATTN_GROUP, seq // HG_SUB, ATTN_GROUP * MLA_V, HG_SUB), BF16),
            jax.ShapeDtypeStruct((n, HG_WIDTH), BF16),
        ],
        scratch_shapes=[
            pltpu.VMEM((HG_HEADS, HG_DV, HG_DK), F32),
            pltpu.VMEM((HG_HEADS, HG_DV, HG_DK), F32),
            pltpu.VMEM((tm, HG_KW), F32),
            pltpu.VMEM((tm, HG_KW), F32),
            pltpu.VMEM((tm, HG_WIDTH), F32),
            pltpu.VMEM((tm, HG_KW), F32),
            pltpu.VMEM((tm, HG_WIDTH), F32),
            pltpu.VMEM((tm, HG_WIDTH), F32),
        ],
        compiler_params=_params(("parallel", "arbitrary")),
        name="mixer_in",
    )(x2, gmix, wcat, gcq, wuq, gckv, wukv, nmask, *tabs, loglb, log1mlb, gout, tri, cmask)


def _merge_kernel(x_ref, ya_ref, yb_ref, gmix_ref, wg_ref, wpa_ref, wpb_ref, wout_ref, o_ref):
    x = x_ref[...]
    h = _rms(x, gmix_ref[...]).astype(BF16)
    ya, yb = ya_ref[...], yb_ref[...]
    acc = x
    w = D_MODEL // MERGE_SPLIT
    for j in range(MERGE_SPLIT):
        cols = slice(j * w, (j + 1) * w)
        g_a = _dot(h, wg_ref[:, cols])
        g_b = _dot(h, wg_ref[:, D_MODEL + j * w:D_MODEL + (j + 1) * w])
        y = (jax.nn.sigmoid(g_a) * _dot(ya, wpa_ref[:, cols])
             + jax.nn.sigmoid(g_b) * _dot(yb, wpb_ref[:, cols]))
        acc = acc + _dot(y.astype(BF16), wout_ref[cols, :])
    o_ref[...] = acc


def _merge(x2, ya, yb, gmix, wg, wpa, wpb, wout, tm):
    n = x2.shape[0]
    return pl.pallas_call(
        _merge_kernel,
        grid=(n // tm,),
        in_specs=[
            pl.BlockSpec((tm, D_MODEL), lambda i: (i, 0)),
            pl.BlockSpec((tm, ya.shape[1]), lambda i: (i, 0)),
            pl.BlockSpec((tm, yb.shape[1]), lambda i: (i, 0)),
            _const_spec(gmix.shape), _const_spec(wg.shape), _const_spec(wpa.shape),
            _const_spec(wpb.shape), _const_spec(wout.shape),
        ],
        out_specs=pl.BlockSpec((tm, D_MODEL), lambda i: (i, 0)),
        out_shape=jax.ShapeDtypeStruct((n, D_MODEL), F32),
        compiler_params=_params(("parallel",)),
        name="merge",
    )(x2, ya, yb, gmix, wg, wpa, wpb, wout)


def _ffn_kernel(x_ref, gffn_ref, wgate_ref, wup_ref, wdown_ref, o_ref, *, n_split):
    x = x_ref[...]
    h = _rms(x, gffn_ref[...]).astype(BF16)
    d_ff = wgate_ref.shape[1]
    w = d_ff // n_split
    acc = x
    for j in range(n_split):
        gate = _dot(h, wgate_ref[:, j * w:(j + 1) * w])
        up = _dot(h, wup_ref[:, j * w:(j + 1) * w])
        act = (gate * jax.nn.sigmoid(gate) * up).astype(BF16)
        acc = acc + _dot(act, wdown_ref[j * w:(j + 1) * w, :])
    o_ref[...] = acc


def _ffn(x2, gffn, wgate, wup, wdown, tm, n_split):
    n = x2.shape[0]
    return pl.pallas_call(
        functools.partial(_ffn_kernel, n_split=n_split),
        grid=(n // tm,),
        in_specs=[
            pl.BlockSpec((tm, D_MODEL), lambda i: (i, 0)),
            _const_spec(gffn.shape), _const_spec(wgate.shape), _const_spec(wup.shape),
            _const_spec(wdown.shape),
        ],
        out_specs=pl.BlockSpec((tm, D_MODEL), lambda i: (i, 0)),
        out_shape=jax.ShapeDtypeStruct((n, D_MODEL), F32),
        compiler_params=_params(("parallel",)),
        name="ffn",
    )(x2, gffn, wgate, wup, wdown)


def _rope_tables(seq, gain, scale):
    pos = jnp.arange(seq, dtype=F32)
    inv_freq = 1.0 / (ROPE_THETA ** (jnp.arange(0, MLA_ROPE, 2, dtype=F32) / MLA_ROPE))
    ang = pos[:, None] * inv_freq[None, :]
    cos, sin = jnp.cos(ang), jnp.sin(ang)
    half = MLA_ROPE // 2
    g_nope, g_rope = gain[:MLA_NOPE], gain[MLA_NOPE:]
    nope = jnp.broadcast_to(g_nope[None, :], (seq, MLA_NOPE))
    c_rope = jnp.concatenate([cos, cos], axis=1) * g_rope[None, :]
    g_partner = jnp.concatenate([g_rope[half:], g_rope[:half]])
    s_rope = jnp.concatenate([-sin, sin], axis=1) * g_partner[None, :]
    zeros = jnp.zeros((seq, MLA_NOPE), F32)
    q_tab = jnp.concatenate([nope, c_rope, s_rope], axis=1) * scale
    ctab = jnp.concatenate([nope, c_rope, c_rope], axis=1) * scale
    stab = jnp.concatenate([zeros, s_rope, s_rope], axis=1) * scale
    return q_tab, ctab, stab


def _layer_params(l, seq, tm_hg, w_in, mla_w_uq, mla_w_ukv, mla_q_norm, mla_k_norm, lower_bounds):
    o = [0]
    for s in IN_SPLITS:
        o.append(o[-1] + s)
    wi = w_in[l]
    half = MLA_ROPE // 2
    z64 = jnp.zeros((D_MODEL, MLA_NOPE), F32)
    w_kpe = wi[:, o[2]:o[3]]
    w_kpe_sw = jnp.concatenate([w_kpe[:, half:], w_kpe[:, :half]], axis=1)
    wmla = jnp.concatenate([wi[:, o[0]:o[2]], z64, w_kpe, w_kpe, z64, w_kpe_sw, w_kpe_sw], axis=1).astype(BF16)
    whg = wi[:, o[3]:o[7]].astype(BF16)
    wgates = wi[:, o[7]:o[9]].astype(BF16)

    uq = mla_w_uq[l].reshape(Q_LORA, MLA_HEADS, MLA_QK)
    uq_nope, uq_rope = uq[..., :MLA_NOPE], uq[..., MLA_NOPE:]
    uq_rope_sw = jnp.concatenate([uq_rope[..., half:], uq_rope[..., :half]], axis=-1)
    wuq = jnp.concatenate([uq_nope, uq_rope, uq_rope_sw], axis=-1).reshape(
        Q_LORA, MLA_HEADS * HEAD_PAD).astype(BF16)

    ukv = mla_w_ukv[l].reshape(KV_LORA, MLA_HEADS, MLA_NOPE + MLA_V)
    zk64 = jnp.zeros((KV_LORA, MLA_HEADS, HEAD_PAD - MLA_NOPE), F32)
    wuk = jnp.concatenate([ukv[..., :MLA_NOPE], zk64], axis=-1).reshape(KV_LORA, MLA_HEADS * HEAD_PAD)
    wuv = ukv[..., MLA_NOPE:].reshape(KV_LORA, MLA_HEADS * MLA_V)
    wukv = jnp.concatenate([wuk, wuv], axis=1).astype(BF16)

    tq_tab, _, _ = _rope_tables(seq, mla_q_norm[l], MLA_QK ** -0.5 * LOG2_E)
    _, ck, sk = _rope_tables(seq, mla_k_norm[l], 1.0)
    norm_mask = (jnp.arange(HEAD_PAD) < MLA_QK).astype(F32)[None, :]

    lb = lower_bounds[l][None, :]
    r = jnp.arange(tm_hg)
    tri = ((r[:, None] // HG_CHUNK == r[None, :] // HG_CHUNK) & (r[:, None] >= r[None, :])).astype(BF16)
    nc = tm_hg // HG_CHUNK
    cmask = (jnp.arange(nc * HG_DV)[:, None] // HG_DV == r[None, :] // HG_CHUNK).astype(BF16)
    return dict(wcat=jnp.concatenate([wmla, whg], axis=1), wgates=wgates, wuq=wuq, wukv=wukv,
                tabs=(tq_tab, ck, sk), norm_mask=norm_mask,
                loglb=jnp.log(lb), log1mlb=jnp.log1p(-lb), tri=tri, cmask=cmask)


def kernel(x, norm_mix, w_in, mla_norm_cq, mla_w_uq, mla_norm_ckv, mla_w_ukv, mla_q_norm, mla_k_norm,
           hg_lb_logits, hg_out_norm, w_proj_a, w_proj_b, w_out, norm_ffn, w_gate, w_up, w_down):
    batch, seq, d = x.shape
    assert d == D_MODEL
    depth = w_in.shape[0]
    tm = min(HG_SUB, seq)
    tm_hg = min(2 * HG_SUB, seq)
    tm_ffn = min(1024, seq)
    assert all(seq % t == 0 for t in (tm, tm_hg, tm_ffn))

    p = jax.nn.softmax(hg_lb_logits.astype(F32), axis=0)
    lower_bounds = jnp.maximum(jnp.cumsum(p, axis=0) - p[0:1], 0.0)

    x2 = x.reshape(batch * seq, d)
    for l in range(depth):
        lp = _layer_params(l, seq, min(HG_SUB, seq), w_in, mla_w_uq, mla_w_ukv, mla_q_norm, mla_k_norm, lower_bounds)
        gmix = norm_mix[l][None, :]
        qt, k, vt, y_b = _mixer_in(x2, gmix, lp["wcat"], mla_norm_cq[l][None, :], lp["wuq"],
                                   mla_norm_ckv[l][None, :], lp["wukv"], lp["norm_mask"], lp["tabs"],
                                   lp["loglb"], lp["log1mlb"], hg_out_norm[l][None, :], lp["tri"], lp["cmask"],
                                   batch, seq, tm_hg)
        y_a = _mla_attn(qt, k, vt, batch, seq, tm)
        x2 = _merge(x2, y_a, y_b, gmix, lp["wgates"], w_proj_a[l].astype(BF16),
                    w_proj_b[l].astype(BF16), w_out[l].astype(BF16), tm_ffn)
        x2 = _ffn(x2, norm_ffn[l][None, :], w_gate[l].astype(BF16), w_up[l].astype(BF16),
                  w_down[l].astype(BF16), tm_ffn, FFN_SPLIT)
    return x2.reshape(batch, seq, d)
```

```python
import functools

import jax
import jax.numpy as jnp
from jax import lax
from jax.experimental import pallas as pl
from jax.experimental.pallas import tpu as pltpu

D_MODEL = 1024
MLA_HEADS = 8
MLA_NOPE = 64
MLA_ROPE = 32
MLA_QK = MLA_NOPE + MLA_ROPE
MLA_V = 64
Q_LORA = 384
KV_LORA = 256
ROPE_THETA = 10000.0
HG_HEADS = 4
HG_DK = 128
HG_DV = 128
HG_KW = HG_HEADS * HG_DK
HG_WIDTH = HG_HEADS * HG_DV
NORM_EPS = 1e-6
IN_SPLITS = (Q_LORA, KV_LORA, MLA_ROPE, HG_KW, HG_KW, HG_WIDTH, HG_WIDTH, D_MODEL, D_MODEL)

LANES = 128
HEAD_PAD = LANES
ATTN_GROUP = 4
Q_PER_STEP = 4
L_ROWS = 16
HG_CHUNK = 32
HG_SUB = 256
HG_SAFE_DECAY = 60.0
LOG2_E = 1.4426950408889634
MLA_IN_W = 896
MERGE_SPLIT = 2
FFN_SPLIT = 11
VMEM_LIMIT = 56 * 1024 * 1024

BF16 = jnp.bfloat16
F32 = jnp.float32
NT = (((1,), (1,)), ((), ()))
TN = (((0,), (0,)), ((), ()))


def _rms(x, g):
    return x * lax.rsqrt(jnp.mean(x * x, axis=-1, keepdims=True) + NORM_EPS) * g


def _dot(a, b):
    return jnp.dot(a, b, preferred_element_type=F32)


def _const_spec(shape):
    nd = len(shape)
    return pl.BlockSpec(shape, lambda *_: (0,) * nd, pipeline_mode=pl.Buffered(1))


def _params(sem):
    return pltpu.CompilerParams(dimension_semantics=sem, vmem_limit_bytes=VMEM_LIMIT)


def _attn_kernel(qt_ref, k_ref, vt_ref, o_ref, s_scr, m_scr, *, tq, n_q):
    step_idx = pl.program_id(2)
    kv_idx = lax.broadcasted_iota(jnp.int32, (tq, tq), 0)
    q_idx = lax.broadcasted_iota(jnp.int32, (tq, tq), 1)
    causal = q_idx >= kv_idx
    ones_rows = jnp.ones((L_ROWS, tq), BF16)

    def scores(j, iq):
        start = pl.multiple_of(j * tq, tq)
        return tuple(_dot(k_ref[pl.ds(start, tq), hh * HEAD_PAD:(hh + 1) * HEAD_PAD],
                          qt_ref[iq, hh * HEAD_PAD:(hh + 1) * HEAD_PAD, :]) for hh in range(ATTN_GROUP))

    def stash(slot, s_all):
        for hh in range(ATTN_GROUP):
            s_scr[slot, hh] = s_all[hh]
            m_scr[slot, hh] = jnp.max(s_all[hh], axis=0, keepdims=True)

    def update(j, slot, carry, masked):
        vt = vt_ref[j]
        out = []
        for hh in range(ATTN_GROUP):
            m, accl = carry[hh]
            s = s_scr[slot, hh]
            if masked:
                s = jnp.where(causal, s, -jnp.inf)
                s_max = jnp.max(s, axis=0, keepdims=True)
            else:
                s_max = m_scr[slot, hh]
            m_new = jnp.maximum(m, s_max)
            alpha = jnp.exp2(m - m_new)
            p = jnp.exp2(s - m_new).astype(BF16)
            lhs = jnp.concatenate([vt[hh * MLA_V:(hh + 1) * MLA_V, :], ones_rows], axis=0)
            out.append((m_new, alpha * accl + _dot(lhs, p)))
        return tuple(out)

    @pl.when(step_idx == 0)
    def _():
        stash(0, scores(0, 0))

    for u in range(Q_PER_STEP):
        i = Q_PER_STEP * step_idx + u

        def step(t, src, dst, carry, i=i):
            s_next = scores(t + 1, i)
            carry = update(t, src, carry, False)
            stash(dst, s_next)
            return carry

        def pair(jj, carry, step=step):
            return step(2 * jj + 1, 1, 0, step(2 * jj, 0, 1, carry))

        init = tuple((jnp.full((1, tq), -jnp.inf, F32), jnp.zeros((MLA_V + L_ROWS, tq), F32))
                     for _ in range(ATTN_GROUP))
        carry = lax.fori_loop(0, (Q_PER_STEP // 2) * step_idx + u // 2, pair, init)
        slot = 0
        if u % 2 == 1:
            carry = step(i - 1, 0, 1, carry)
            slot = 1
        s_ahead = scores(0, jnp.minimum(i + 1, n_q - 1))
        carry = update(i, slot, carry, True)
        stash(0, s_ahead)
        o_t = jnp.concatenate([accl[:MLA_V] / accl[MLA_V:MLA_V + 1] for _, accl in carry], axis=0)
        o_ref[u * tq:(u + 1) * tq, :] = o_t.T.astype(BF16)


def _mla_attn(qt, k, vt, batch, seq, tq):
    nq = seq // tq
    assert nq % Q_PER_STEP == 0 and Q_PER_STEP % 2 == 0
    return pl.pallas_call(
        functools.partial(_attn_kernel, tq=tq, n_q=nq),
        grid=(batch, MLA_HEADS // ATTN_GROUP, nq // Q_PER_STEP),
        in_specs=[
            pl.BlockSpec((None, nq, ATTN_GROUP * HEAD_PAD, tq), lambda b, g, i: (b, 0, g, 0)),
            pl.BlockSpec((seq, ATTN_GROUP * HEAD_PAD), lambda b, g, i: (b, g)),
            pl.BlockSpec((None, None, nq, ATTN_GROUP * MLA_V, tq), lambda b, g, i: (b, g, 0, 0, 0)),
        ],
        out_specs=pl.BlockSpec((Q_PER_STEP * tq, ATTN_GROUP * MLA_V),
                               lambda b, g, i: (b * (nq // Q_PER_STEP) + i, g)),
        out_shape=jax.ShapeDtypeStruct((batch * seq, MLA_HEADS * MLA_V), BF16),
        scratch_shapes=[
            pltpu.VMEM((2, ATTN_GROUP, tq, tq), F32),
            pltpu.VMEM((2, ATTN_GROUP, 1, tq), F32),
        ],
        compiler_params=_params(("parallel", "parallel", "arbitrary")),
        name="mla_attn",
    )(qt, k, vt)


def _split3(x):
    a = x.astype(BF16)
    r = x - a.astype(F32)
    b = r.astype(BF16)
    c = (r - b.astype(F32)).astype(BF16)
    return a, b, c


def _mixer_in_kernel(x_ref, gmix_ref, wcat_ref, gcq_ref, wuq_ref, gckv_ref, wukv_ref, nmask_ref,
                     tq_ref, ck_ref, sk_ref, loglb_ref, log1mlb_ref, gout_ref, tri_ref, cmask_ref,
                     qt_ref, k_ref, vt_ref, o_ref,
                     st_ref, stbak_ref, q_scr, k_scr, v_scr, b_scr, o_scr, g_scr, *, tm):
    S = HG_SUB
    C = HG_CHUNK
    n_sub = tm // S
    ncs = S // C
    nc = tm // C

    @pl.when(pl.program_id(1) == 0)
    def _():
        st_ref[...] = jnp.zeros_like(st_ref)

    stbak_ref[...] = st_ref[...]
    gmix = gmix_ref[...]
    zzs = [_dot(_rms(x_ref[u * S:(u + 1) * S, :], gmix).astype(BF16), wcat_ref[...]) for u in range(n_sub)]
    zs = [zz[:, MLA_IN_W:] for zz in zzs]
    nmask = nmask_ref[...]
    inv_qk = 1.0 / MLA_QK
    gw = ATTN_GROUP * MLA_V

    tri = tri_ref[...]
    cmask = cmask_ref[...]
    row = lax.broadcasted_iota(jnp.int32, (S, S), 0)
    col = lax.broadcasted_iota(jnp.int32, (S, S), 1)
    causal = (row >= col) & (row // C == col // C)
    a = loglb_ref[...]
    log1mlb = log1mlb_ref[...]
    sts = [st_ref[hd] for hd in range(HG_HEADS)]
    b_mins = []
    for u in range(n_sub):
        rows = slice(u * S, (u + 1) * S)
        zc = zzs[u][:, :MLA_IN_W]
        c_q = zc[:, :Q_LORA]
        c_kv = zc[:, Q_LORA:Q_LORA + KV_LORA]
        kpe_a = zc[:, 640:768]
        kpe_b = zc[:, 768:896]
        q2 = _dot(_rms(c_q, gcq_ref[...]).astype(BF16), wuq_ref[...])
        kv2 = _dot(_rms(c_kv, gckv_ref[...]).astype(BF16), wukv_ref[...])
        tq_tab, ck, sk = tq_ref[rows, :], ck_ref[rows, :], sk_ref[rows, :]
        for hd in range(MLA_HEADS):
            lo = hd * HEAD_PAD
            qa = q2[:, lo:lo + HEAD_PAD]
            rq = lax.rsqrt(jnp.sum(qa * qa * nmask, axis=-1, keepdims=True) * inv_qk + NORM_EPS)
            qt_ref[u, lo:lo + HEAD_PAD, :] = (qa * tq_tab * rq).astype(BF16).T
            ka = kv2[:, lo:lo + HEAD_PAD] + kpe_a
            rk = lax.rsqrt(jnp.sum(ka * ka * nmask, axis=-1, keepdims=True) * inv_qk + NORM_EPS)
            k_ref[rows, lo:lo + HEAD_PAD] = ((ka * ck + kpe_b * sk) * rk).astype(BF16)
        v = kv2[:, MLA_HEADS * HEAD_PAD:]
        for g in range(MLA_HEADS // ATTN_GROUP):
            vt_ref[g, u] = v[:, g * gw:(g + 1) * gw].astype(BF16).T

        z = zs[u]
        zq = z[:, :HG_KW]
        zf = z[:, HG_KW:2 * HG_KW]
        zi = z[:, 2 * HG_KW:2 * HG_KW + HG_WIDTH]
        zg = z[:, 2 * HG_KW + HG_WIDTH:]
        log_sig = jnp.minimum(zf, 0.0) - jnp.log(1.0 + jnp.exp(-jnp.abs(zf)))
        c = log1mlb + log_sig
        log_f = jnp.maximum(a, c) + jnp.log(1.0 + jnp.exp(-jnp.abs(a - c)))
        kk = 1.0 - jnp.exp(log_f)
        l1, l2, l3 = _split3(log_f)
        b = _dot(tri, l1) + _dot(tri, l2) + _dot(tri, l3)
        b3 = b.reshape(ncs, C, HG_KW)
        b_last = b3[:, C - 1:C, :]
        q_scr[rows, :] = zq
        k_scr[rows, :] = kk
        v_scr[rows, :] = zi
        b_scr[rows, :] = b
        g_scr[rows, :] = zg * jax.nn.sigmoid(zg)
        b_mins.append(jnp.min(b))

        qe = (zq * jnp.exp(b)).astype(BF16)
        ke = (kk * jnp.exp(jnp.minimum(-b, HG_SAFE_DECAY + 20.0))).astype(BF16)
        kl = (kk.reshape(ncs, C, HG_KW) * jnp.exp(b_last - b3)).reshape(S, HG_KW).astype(BF16)
        e_last = jnp.exp(b_last)
        vb = zi.astype(BF16)
        vt = vb.T
        for hd in range(HG_HEADS):
            lo = hd * HG_DK
            qe_h = qe[:, lo:lo + HG_DK]
            att = lax.dot_general(qe_h, ke[:, lo:lo + HG_DK], NT, preferred_element_type=F32)
            att = jnp.where(causal, att, 0.0).astype(BF16)
            o_intra = _dot(att, vb[:, lo:lo + HG_DV])
            vt_h = vt[lo:lo + HG_DV, :]
            upd = _dot(jnp.concatenate([vt_h] * ncs, axis=0) * cmask, kl[:, lo:lo + HG_DK])
            st = sts[hd]
            for ci in range(ncs):
                r0 = ci * C
                o_inter = lax.dot_general(qe_h[r0:r0 + C], st.astype(BF16), NT, preferred_element_type=F32)
                o_scr[u * S + r0:u * S + r0 + C, lo:lo + HG_DV] = o_intra[r0:r0 + C] + o_inter
                st = st * e_last[ci, :, lo:lo + HG_DK] + upd[ci * HG_DV:(ci + 1) * HG_DV]
            sts[hd] = st
    for hd in range(HG_HEADS):
        st_ref[hd] = sts[hd]
    b_min = b_mins[0]
    for u in range(1, n_sub):
        b_min = jnp.minimum(b_min, b_mins[u])
    unsafe = b_min < -HG_SAFE_DECAY

    @pl.when(unsafe)
    def _():
        st_ref[...] = stbak_ref[...]
        t_idx = lax.broadcasted_iota(jnp.int32, (C, 1), 0)
        for hd in range(HG_HEADS):
            lo = hd * HG_DK

            def chunk_body(ci, _, lo=lo, hd=hd):
                r0 = pl.multiple_of(ci * C, C)
                q_c = q_scr[pl.ds(r0, C), lo:lo + HG_DK]
                k_c = k_scr[pl.ds(r0, C), lo:lo + HG_DK]
                v_c = v_scr[pl.ds(r0, C), lo:lo + HG_DV]
                b_c = b_scr[pl.ds(r0, C), lo:lo + HG_DK]
                bl = b_c[C - 1:C, :]

                def s_body(s, o_acc):
                    pick = t_idx == s
                    ks = jnp.sum(jnp.where(pick, k_c, 0.0), axis=0, keepdims=True)
                    bs = jnp.sum(jnp.where(pick, b_c, 0.0), axis=0, keepdims=True)
                    vs = jnp.sum(jnp.where(pick, v_c, 0.0), axis=0, keepdims=True)
                    dec = jnp.exp(jnp.minimum(b_c - bs, 0.0))
                    w = jnp.sum(q_c * ks * dec, axis=-1, keepdims=True)
                    w = jnp.where(t_idx >= s, w, 0.0)
                    return o_acc + w * vs

                o_intra = lax.fori_loop(0, C, s_body, jnp.zeros((C, HG_DV), F32))
                st = st_ref[hd]
                qe_c = (q_c * jnp.exp(b_c)).astype(BF16)
                o_inter = lax.dot_general(qe_c, st.astype(BF16), NT, preferred_element_type=F32)
                o_scr[pl.ds(r0, C), lo:lo + HG_DV] = o_intra + o_inter
                kl_c = (k_c * jnp.exp(bl - b_c)).astype(BF16)
                upd = lax.dot_general(v_c.astype(BF16), kl_c, TN, preferred_element_type=F32)
                st_ref[hd] = st * jnp.exp(bl) + upd
                return 0

            lax.fori_loop(0, nc, chunk_body, 0)

    o = o_scr[...]
    gate = g_scr[...]
    gout = gout_ref[...]
    for hd in range(HG_HEADS):
        lo = hd * HG_DV
        o_ref[:, lo:lo + HG_DV] = (_rms(o[:, lo:lo + HG_DV], gout) * gate[:, lo:lo + HG_DV]).astype(BF16)


def _mixer_in(x2, gmix, wcat, gcq, wuq, gckv, wukv, nmask, tabs, loglb, log1mlb, gout, tri, cmask,
              batch, seq, tm):
    nt = seq // tm
    n_sub = tm // HG_SUB
    n = batch * seq
    tab_spec = pl.BlockSpec((tm, LANES), lambda b, i: (i, 0))

    def row_spec(width):
        return pl.BlockSpec((tm, width), lambda b, i: (b * nt + i, 0))

    return pl.pallas_call(
        functools.partial(_mixer_in_kernel, tm=tm),
        grid=(batch, nt),
        in_specs=[
            row_spec(D_MODEL),
            _const_spec(gmix.shape), _const_spec(wcat.shape), _const_spec(gcq.shape),
            _const_spec(wuq.shape), _const_spec(gckv.shape), _const_spec(wukv.shape),
            _const_spec(nmask.shape), tab_spec, tab_spec, tab_spec,
            _const_spec(loglb.shape), _const_spec(log1mlb.shape), _const_spec(gout.shape),
            _const_spec(tri.shape), _const_spec(cmask.shape),
        ],
        out_specs=[
            pl.BlockSpec((None, n_sub, MLA_HEADS * HEAD_PAD, HG_SUB), lambda b, i: (b, i, 0, 0)),
            row_spec(MLA_HEADS * HEAD_PAD),
            pl.BlockSpec((None, MLA_HEADS // ATTN_GROUP, n_sub, ATTN_GROUP * MLA_V, HG_SUB),
                         lambda b, i: (b, 0, i, 0, 0)),
            row_spec(HG_WIDTH),
        ],
        out_shape=[
            jax.ShapeDtypeStruct((batch, seq // HG_SUB, MLA_HEADS * HEAD_PAD, HG_SUB), BF16),
            jax.ShapeDtypeStruct((n, MLA_HEADS * HEAD_PAD), BF16),
            jax.ShapeDtypeStruct((batch, MLA_HEADS // ATTN_GROUP, seq // HG_SUB, ATTN_GROUP * MLA_V, HG_SUB), BF16),
            jax.ShapeDtypeStruct((n, HG_WIDTH), BF16),
        ],
        scratch_shapes=[
            pltpu.VMEM((HG_HEADS, HG_DV, HG_DK), F32),
            pltpu.VMEM((HG_HEADS, HG_DV, HG_DK), F32),
            pltpu.VMEM((tm, HG_KW), F32),
            pltpu.VMEM((tm, HG_KW), F32),
            pltpu.VMEM((tm, HG_WIDTH), F32),
            pltpu.VMEM((tm, HG_KW), F32),
            pltpu.VMEM((tm, HG_WIDTH), F32),
            pltpu.VMEM((tm, HG_WIDTH), F32),
        ],
        compiler_params=_params(("parallel", "arbitrary")),
        name="mixer_in",
    )(x2, gmix, wcat, gcq, wuq, gckv, wukv, nmask, *tabs, loglb, log1mlb, gout, tri, cmask)


def _merge_kernel(x_ref, ya_ref, yb_ref, gmix_ref, wg_ref, wpa_ref, wpb_ref, wout_ref, o_ref):
    x = x_ref[...]
    h = _rms(x, gmix_ref[...]).astype(BF16)
    ya, yb = ya_ref[...], yb_ref[...]
    acc = x
    w = D_MODEL // MERGE_SPLIT
    for j in range(MERGE_SPLIT):
        cols = slice(j * w, (j + 1) * w)
        g_a = _dot(h, wg_ref[:, cols])
        g_b = _dot(h, wg_ref[:, D_MODEL + j * w:D_MODEL + (j + 1) * w])
        y = (jax.nn.sigmoid(g_a) * _dot(ya, wpa_ref[:, cols])
             + jax.nn.sigmoid(g_b) * _dot(yb, wpb_ref[:, cols]))
        acc = acc + _dot(y.astype(BF16), wout_ref[cols, :])
    o_ref[...] = acc


def _merge(x2, ya, yb, gmix, wg, wpa, wpb, wout, tm):
    n = x2.shape[0]
    return pl.pallas_call(
        _merge_kernel,
        grid=(n // tm,),
        in_specs=[
            pl.BlockSpec((tm, D_MODEL), lambda i: (i, 0)),
            pl.BlockSpec((tm, ya.shape[1]), lambda i: (i, 0)),
            pl.BlockSpec((tm, yb.shape[1]), lambda i: (i, 0)),
            _const_spec(gmix.shape), _const_spec(wg.shape), _const_spec(wpa.shape),
            _const_spec(wpb.shape), _const_spec(wout.shape),
        ],
        out_specs=pl.BlockSpec((tm, D_MODEL), lambda i: (i, 0)),
        out_shape=jax.ShapeDtypeStruct((n, D_MODEL), F32),
        compiler_params=_params(("parallel",)),
        name="merge",
    )(x2, ya, yb, gmix, wg, wpa, wpb, wout)


def _ffn_kernel(x_ref, gffn_ref, wgate_ref, wup_ref, wdown_ref, o_ref, *, n_split):
    x = x_ref[...]
    h = _rms(x, gffn_ref[...]).astype(BF16)
    d_ff = wgate_ref.shape[1]
    w = d_ff // n_split
    acc = x
    for j in range(n_split):
        gate = _dot(h, wgate_ref[:, j * w:(j + 1) * w])
        up = _dot(h, wup_ref[:, j * w:(j + 1) * w])
        act = (gate * jax.nn.sigmoid(gate) * up).astype(BF16)
        acc = acc + _dot(act, wdown_ref[j * w:(j + 1) * w, :])
    o_ref[...] = acc


def _ffn(x2, gffn, wgate, wup, wdown, tm, n_split):
    n = x2.shape[0]
    return pl.pallas_call(
        functools.partial(_ffn_kernel, n_split=n_split),
        grid=(n // tm,),
        in_specs=[
            pl.BlockSpec((tm, D_MODEL), lambda i: (i, 0)),
            _const_spec(gffn.shape), _const_spec(wgate.shape), _const_spec(wup.shape),
            _const_spec(wdown.shape),
        ],
        out_specs=pl.BlockSpec((tm, D_MODEL), lambda i: (i, 0)),
        out_shape=jax.ShapeDtypeStruct((n, D_MODEL), F32),
        compiler_params=_params(("parallel",)),
        name="ffn",
    )(x2, gffn, wgate, wup, wdown)


def _rope_tables(seq, gain, scale):
    pos = jnp.arange(seq, dtype=F32)
    inv_freq = 1.0 / (ROPE_THETA ** (jnp.arange(0, MLA_ROPE, 2, dtype=F32) / MLA_ROPE))
    ang = pos[:, None] * inv_freq[None, :]
    cos, sin = jnp.cos(ang), jnp.sin(ang)
    half = MLA_ROPE // 2
    g_nope, g_rope = gain[:MLA_NOPE], gain[MLA_NOPE:]
    nope = jnp.broadcast_to(g_nope[None, :], (seq, MLA_NOPE))
    c_rope = jnp.concatenate([cos, cos], axis=1) * g_rope[None, :]
    g_partner = jnp.concatenate([g_rope[half:], g_rope[:half]])
    s_rope = jnp.concatenate([-sin, sin], axis=1) * g_partner[None, :]
    zeros = jnp.zeros((seq, MLA_NOPE), F32)
    q_tab = jnp.concatenate([nope, c_rope, s_rope], axis=1) * scale
    ctab = jnp.concatenate([nope, c_rope, c_rope], axis=1) * scale
    stab = jnp.concatenate([zeros, s_rope, s_rope], axis=1) * scale
    return q_tab, ctab, stab


def _layer_params(l, seq, tm_hg, w_in, mla_w_uq, mla_w_ukv, mla_q_norm, mla_k_norm, lower_bounds):
    o = [0]
    for s in IN_SPLITS:
        o.append(o[-1] + s)
    wi = w_in[l]
    half = MLA_ROPE // 2
    z64 = jnp.zeros((D_MODEL, MLA_NOPE), F32)
    w_kpe = wi[:, o[2]:o[3]]
    w_kpe_sw = jnp.concatenate([w_kpe[:, half:], w_kpe[:, :half]], axis=1)
    wmla = jnp.concatenate([wi[:, o[0]:o[2]], z64, w_kpe, w_kpe, z64, w_kpe_sw, w_kpe_sw], axis=1).astype(BF16)
    whg = wi[:, o[3]:o[7]].astype(BF16)
    wgates = wi[:, o[7]:o[9]].astype(BF16)

    uq = mla_w_uq[l].reshape(Q_LORA, MLA_HEADS, MLA_QK)
    uq_nope, uq_rope = uq[..., :MLA_NOPE], uq[..., MLA_NOPE:]
    uq_rope_sw = jnp.concatenate([uq_rope[..., half:], uq_rope[..., :half]], axis=-1)
    wuq = jnp.concatenate([uq_nope, uq_rope, uq_rope_sw], axis=-1).reshape(
        Q_LORA, MLA_HEADS * HEAD_PAD).astype(BF16)

    ukv = mla_w_ukv[l].reshape(KV_LORA, MLA_HEADS, MLA_NOPE + MLA_V)
    zk64 = jnp.zeros((KV_LORA, MLA_HEADS, HEAD_PAD - MLA_NOPE), F32)
    wuk = jnp.concatenate([ukv[..., :MLA_NOPE], zk64], axis=-1).reshape(KV_LORA, MLA_HEADS * HEAD_PAD)
    wuv = ukv[..., MLA_NOPE:].reshape(KV_LORA, MLA_HEADS * MLA_V)
    wukv = jnp.concatenate([wuk, wuv], axis=1).astype(BF16)

    tq_tab, _, _ = _rope_tables(seq, mla_q_norm[l], MLA_QK ** -0.5 * LOG2_E)
    _, ck, sk = _rope_tables(seq, mla_k_norm[l], 1.0)
    norm_mask = (jnp.arange(HEAD_PAD) < MLA_QK).astype(F32)[None, :]

    lb = lower_bounds[l][None, :]
    r = jnp.arange(tm_hg)
    tri = ((r[:, None] // HG_CHUNK == r[None, :] // HG_CHUNK) & (r[:, None] >= r[None, :])).astype(BF16)
    nc = tm_hg // HG_CHUNK
    cmask = (jnp.arange(nc * HG_DV)[:, None] // HG_DV == r[None, :] // HG_CHUNK).astype(BF16)
    return dict(wcat=jnp.concatenate([wmla, whg], axis=1), wgates=wgates, wuq=wuq, wukv=wukv,
                tabs=(tq_tab, ck, sk), norm_mask=norm_mask,
                loglb=jnp.log(lb), log1mlb=jnp.log1p(-lb), tri=tri, cmask=cmask)


def kernel(x, norm_mix, w_in, mla_norm_cq, mla_w_uq, mla_norm_ckv, mla_w_ukv, mla_q_norm, mla_k_norm,
           hg_lb_logits, hg_out_norm, w_proj_a, w_proj_b, w_out, norm_ffn, w_gate, w_up, w_down):
    batch, seq, d = x.shape
    assert d == D_MODEL
    depth = w_in.shape[0]
    tm = min(HG_SUB, seq)
    tm_hg = min(2 * HG_SUB, seq)
    tm_ffn = min(1024, seq)
    assert all(seq % t == 0 for t in (tm, tm_hg, tm_ffn))

    p = jax.nn.softmax(hg_lb_logits.astype(F32), axis=0)
    lower_bounds = jnp.maximum(jnp.cumsum(p, axis=0) - p[0:1], 0.0)

    x2 = x.reshape(batch * seq, d)
    for l in range(depth):
        lp = _layer_params(l, seq, min(HG_SUB, seq), w_in, mla_w_uq, mla_w_ukv, mla_q_norm, mla_k_norm, lower_bounds)
        gmix = norm_mix[l][None, :]
        qt, k, vt, y_b = _mixer_in(x2, gmix, lp["wcat"], mla_norm_cq[l][None, :], lp["wuq"],
                                   mla_norm_ckv[l][None, :], lp["wukv"], lp["norm_mask"], lp["tabs"],
                                   lp["loglb"], lp["log1mlb"], hg_out_norm[l][None, :], lp["tri"], lp["cmask"],
                                   batch, seq, tm_hg)
        y_a = _mla_attn(qt, k, vt, batch, seq, tm)
        x2 = _merge(x2, y_a, y_b, gmix, lp["wgates"], w_proj_a[l].astype(BF16),
                    w_proj_b[l].astype(BF16), w_out[l].astype(BF16), tm_ffn)
        x2 = _ffn(x2, norm_ffn[l][None, :], w_gate[l].astype(BF16), w_up[l].astype(BF16),
                  w_down[l].astype(BF16), tm_ffn, FFN_SPLIT)
    return x2.reshape(batch, seq, d)
```

```python
import functools

import jax
import jax.numpy as jnp
from jax import lax
from jax.experimental import pallas as pl
from jax.experimental.pallas import tpu as pltpu

D_MODEL = 1024
MLA_HEADS = 8
MLA_NOPE = 64
MLA_ROPE = 32
MLA_QK = MLA_NOPE + MLA_ROPE
MLA_V = 64
Q_LORA = 384
KV_LORA = 256
ROPE_THETA = 10000.0
HG_HEADS = 4
HG_DK = 128
HG_DV = 128
HG_KW = HG_HEADS * HG_DK
HG_WIDTH = HG_HEADS * HG_DV
NORM_EPS = 1e-6
IN_SPLITS = (Q_LORA, KV_LORA, MLA_ROPE, HG_KW, HG_KW, HG_WIDTH, HG_WIDTH, D_MODEL, D_MODEL)

LANES = 128
HEAD_PAD = LANES
ATTN_GROUP = 4
Q_PER_STEP = 4
L_ROWS = 16
HG_CHUNK = 32
HG_SUB = 256
HG_SAFE_DECAY = 60.0
LOG2_E = 1.4426950408889634
KPE_OFF = Q_LORA + KV_LORA
MLA_IN_W = KPE_OFF + 2 * LANES
MERGE_SPLIT = 2
FFN_SPLIT = 11
VMEM_LIMIT = 56 * 1024 * 1024

BF16 = jnp.bfloat16
F32 = jnp.float32
NT = (((1,), (1,)), ((), ()))
TN = (((0,), (0,)), ((), ()))


def _rms(x, g):
    return x * lax.rsqrt(jnp.mean(x * x, axis=-1, keepdims=True) + NORM_EPS) * g


def _dot(a, b):
    return jnp.dot(a, b, preferred_element_type=F32)


def _const_spec(shape):
    nd = len(shape)
    return pl.BlockSpec(shape, lambda *_: (0,) * nd, pipeline_mode=pl.Buffered(1))


def _params(sem):
    return pltpu.CompilerParams(dimension_semantics=sem, vmem_limit_bytes=VMEM_LIMIT)


def _attn_kernel(qt_ref, k_ref, vt_ref, o_ref, s_scr, m_scr, *, tq, n_q):
    step_idx = pl.program_id(2)
    kv_idx = lax.broadcasted_iota(jnp.int32, (tq, tq), 0)
    q_idx = lax.broadcasted_iota(jnp.int32, (tq, tq), 1)
    causal = q_idx >= kv_idx
    ones_rows = jnp.ones((L_ROWS, tq), BF16)

    def scores(j, iq):
        start = pl.multiple_of(j * tq, tq)
        return tuple(_dot(k_ref[pl.ds(start, tq), hh * HEAD_PAD:(hh + 1) * HEAD_PAD],
                          qt_ref[iq, hh * HEAD_PAD:(hh + 1) * HEAD_PAD, :]) for hh in range(ATTN_GROUP))

    def stash(slot, s_all):
        for hh in range(ATTN_GROUP):
            s_scr[slot, hh] = s_all[hh]
            m_scr[slot, hh] = jnp.max(s_all[hh], axis=0, keepdims=True)

    def update(j, slot, carry, masked):
        vt = vt_ref[j]
        out = []
        for hh in range(ATTN_GROUP):
            m, accl = carry[hh]
            s = s_scr[slot, hh]
            if masked:
                s = jnp.where(causal, s, -jnp.inf)
                s_max = jnp.max(s, axis=0, keepdims=True)
            else:
                s_max = m_scr[slot, hh]
            m_new = jnp.maximum(m, s_max)
            alpha = jnp.exp2(m - m_new)
            p = jnp.exp2(s - m_new).astype(BF16)
            lhs = jnp.concatenate([vt[hh * MLA_V:(hh + 1) * MLA_V, :], ones_rows], axis=0)
            out.append((m_new, alpha * accl + _dot(lhs, p)))
        return tuple(out)

    @pl.when(step_idx == 0)
    def _():
        stash(0, scores(0, 0))

    for u in range(Q_PER_STEP):
        i = Q_PER_STEP * step_idx + u

        def step(t, src, dst, carry, i=i):
            s_next = scores(t + 1, i)
            carry = update(t, src, carry, False)
            stash(dst, s_next)
            return carry

        def pair(jj, carry, step=step):
            return step(2 * jj + 1, 1, 0, step(2 * jj, 0, 1, carry))

        init = tuple((jnp.full((1, tq), -jnp.inf, F32), jnp.zeros((MLA_V + L_ROWS, tq), F32))
                     for _ in range(ATTN_GROUP))
        carry = lax.fori_loop(0, (Q_PER_STEP // 2) * step_idx + u // 2, pair, init)
        slot = 0
        if u % 2 == 1:
            carry = step(i - 1, 0, 1, carry)
            slot = 1
        s_ahead = scores(0, jnp.minimum(i + 1, n_q - 1))
        carry = update(i, slot, carry, True)
        stash(0, s_ahead)
        o_t = jnp.concatenate([accl[:MLA_V] / accl[MLA_V:MLA_V + 1] for _, accl in carry], axis=0)
        o_ref[u * tq:(u + 1) * tq, :] = o_t.T.astype(BF16)


def _mla_attn(qt, k, vt, batch, seq, tq):
    nq = seq // tq
    assert nq % Q_PER_STEP == 0 and Q_PER_STEP % 2 == 0
    return pl.pallas_call(
        functools.partial(_attn_kernel, tq=tq, n_q=nq),
        grid=(batch, MLA_HEADS // ATTN_GROUP, nq // Q_PER_STEP),
        in_specs=[
            pl.BlockSpec((None, nq, ATTN_GROUP * HEAD_PAD, tq), lambda b, g, i: (b, 0, g, 0)),
            pl.BlockSpec((seq, ATTN_GROUP * HEAD_PAD), lambda b, g, i: (b, g)),
            pl.BlockSpec((None, None, nq, ATTN_GROUP * MLA_V, tq), lambda b, g, i: (b, g, 0, 0, 0)),
        ],
        out_specs=pl.BlockSpec((Q_PER_STEP * tq, ATTN_GROUP * MLA_V),
                               lambda b, g, i: (b * (nq // Q_PER_STEP) + i, g)),
        out_shape=jax.ShapeDtypeStruct((batch * seq, MLA_HEADS * MLA_V), BF16),
        scratch_shapes=[
            pltpu.VMEM((2, ATTN_GROUP, tq, tq), F32),
            pltpu.VMEM((2, ATTN_GROUP, 1, tq), F32),
        ],
        compiler_params=_params(("parallel", "parallel", "arbitrary")),
        name="mla_attn",
    )(qt, k, vt)


def _split3(x):
    a = x.astype(BF16)
    r = x - a.astype(F32)
    b = r.astype(BF16)
    c = (r - b.astype(F32)).astype(BF16)
    return a, b, c


def _mixer_in_kernel(x_ref, gmix_ref, wcat_ref, gcq_ref, wuq_ref, gckv_ref, wukv_ref, nmask_ref,
                     tq_ref, ck_ref, sk_ref, loglb_ref, log1mlb_ref, gout_ref, tri_ref, cmask_ref,
                     qt_ref, k_ref, vt_ref, o_ref,
                     st_ref, stbak_ref, q_scr, k_scr, v_scr, b_scr, o_scr, g_scr, *, tm):
    S = HG_SUB
    C = HG_CHUNK
    n_sub = tm // S
    ncs = S // C
    nc = tm // C

    @pl.when(pl.program_id(1) == 0)
    def _():
        st_ref[...] = jnp.zeros_like(st_ref)

    stbak_ref[...] = st_ref[...]
    gmix = gmix_ref[...]
    zzs = [_dot(_rms(x_ref[u * S:(u + 1) * S, :], gmix).astype(BF16), wcat_ref[...]) for u in range(n_sub)]
    zs = [zz[:, MLA_IN_W:] for zz in zzs]
    nmask = nmask_ref[...]
    inv_qk = 1.0 / MLA_QK
    gw = ATTN_GROUP * MLA_V

    tri = tri_ref[...]
    cmask = cmask_ref[...]
    row = lax.broadcasted_iota(jnp.int32, (S, S), 0)
    col = lax.broadcasted_iota(jnp.int32, (S, S), 1)
    causal = (row >= col) & (row // C == col // C)
    a = loglb_ref[...]
    log1mlb = log1mlb_ref[...]
    sts = [st_ref[hd] for hd in range(HG_HEADS)]
    b_mins = []
    for u in range(n_sub):
        rows = slice(u * S, (u + 1) * S)
        zc = zzs[u][:, :MLA_IN_W]
        c_q = zc[:, :Q_LORA]
        c_kv = zc[:, Q_LORA:KPE_OFF]
        kpe_a = zc[:, KPE_OFF:KPE_OFF + LANES]
        kpe_b = zc[:, KPE_OFF + LANES:]
        q2 = _dot(_rms(c_q, gcq_ref[...]).astype(BF16), wuq_ref[...])
        kv2 = _dot(_rms(c_kv, gckv_ref[...]).astype(BF16), wukv_ref[...])
        tq_tab, ck, sk = tq_ref[rows, :], ck_ref[rows, :], sk_ref[rows, :]
        for hd in range(MLA_HEADS):
            lo = hd * HEAD_PAD
            qa = q2[:, lo:lo + HEAD_PAD]
            rq = lax.rsqrt(jnp.sum(qa * qa * nmask, axis=-1, keepdims=True) * inv_qk + NORM_EPS)
            qt_ref[u, lo:lo + HEAD_PAD, :] = (qa * tq_tab * rq).astype(BF16).T
            ka = kv2[:, lo:lo + HEAD_PAD] + kpe_a
            rk = lax.rsqrt(jnp.sum(ka * ka * nmask, axis=-1, keepdims=True) * inv_qk + NORM_EPS)
            k_ref[rows, lo:lo + HEAD_PAD] = ((ka * ck + kpe_b * sk) * rk).astype(BF16)
        v = kv2[:, MLA_HEADS * HEAD_PAD:]
        for g in range(MLA_HEADS // ATTN_GROUP):
            vt_ref[g, u] = v[:, g * gw:(g + 1) * gw].astype(BF16).T

        z = zs[u]
        zq = z[:, :HG_KW]
        zf = z[:, HG_KW:2 * HG_KW]
        zi = z[:, 2 * HG_KW:2 * HG_KW + HG_WIDTH]
        zg = z[:, 2 * HG_KW + HG_WIDTH:]
        log_sig = jnp.minimum(zf, 0.0) - jnp.log(1.0 + jnp.exp(-jnp.abs(zf)))
        c = log1mlb + log_sig
        log_f = jnp.maximum(a, c) + jnp.log(1.0 + jnp.exp(-jnp.abs(a - c)))
        kk = 1.0 - jnp.exp(log_f)
        l1, l2, l3 = _split3(log_f)
        b = _dot(tri, l1) + _dot(tri, l2) + _dot(tri, l3)
        b3 = b.reshape(ncs, C, HG_KW)
        b_last = b3[:, C - 1:C, :]
        q_scr[rows, :] = zq
        k_scr[rows, :] = kk
        v_scr[rows, :] = zi
        b_scr[rows, :] = b
        g_scr[rows, :] = zg * jax.nn.sigmoid(zg)
        b_mins.append(jnp.min(b))

        qe = (zq * jnp.exp(b)).astype(BF16)
        ke = (kk * jnp.exp(jnp.minimum(-b, HG_SAFE_DECAY + 20.0))).astype(BF16)
        kl = (kk.reshape(ncs, C, HG_KW) * jnp.exp(b_last - b3)).reshape(S, HG_KW).astype(BF16)
        e_last = jnp.exp(b_last)
        vb = zi.astype(BF16)
        vt = vb.T
        for hd in range(HG_HEADS):
            lo = hd * HG_DK
            qe_h = qe[:, lo:lo + HG_DK]
            att = lax.dot_general(qe_h, ke[:, lo:lo + HG_DK], NT, preferred_element_type=F32)
            att = jnp.where(causal, att, 0.0).astype(BF16)
            o_intra = _dot(att, vb[:, lo:lo + HG_DV])
            vt_h = vt[lo:lo + HG_DV, :]
            upd = _dot(jnp.concatenate([vt_h] * ncs, axis=0) * cmask, kl[:, lo:lo + HG_DK])
            st = sts[hd]
            for ci in range(ncs):
                r0 = ci * C
                o_inter = lax.dot_general(qe_h[r0:r0 + C], st.astype(BF16), NT, preferred_element_type=F32)
                o_scr[u * S + r0:u * S + r0 + C, lo:lo + HG_DV] = o_intra[r0:r0 + C] + o_inter
                st = st * e_last[ci, :, lo:lo + HG_DK] + upd[ci * HG_DV:(ci + 1) * HG_DV]
            sts[hd] = st
    for hd in range(HG_HEADS):
        st_ref[hd] = sts[hd]
    b_min = b_mins[0]
    for u in range(1, n_sub):
        b_min = jnp.minimum(b_min, b_mins[u])
    unsafe = b_min < -HG_SAFE_DECAY

    @pl.when(unsafe)
    def _():
        st_ref[...] = stbak_ref[...]
        t_idx = lax.broadcasted_iota(jnp.int32, (C, 1), 0)
        for hd in range(HG_HEADS):
            lo = hd * HG_DK

            def chunk_body(ci, _, lo=lo, hd=hd):
                r0 = pl.multiple_of(ci * C, C)
                q_c = q_scr[pl.ds(r0, C), lo:lo + HG_DK]
                k_c = k_scr[pl.ds(r0, C), lo:lo + HG_DK]
                v_c = v_scr[pl.ds(r0, C), lo:lo + HG_DV]
                b_c = b_scr[pl.ds(r0, C), lo:lo + HG_DK]
                bl = b_c[C - 1:C, :]

                def s_body(s, o_acc):
                    pick = t_idx == s
                    ks = jnp.sum(jnp.where(pick, k_c, 0.0), axis=0, keepdims=True)
                    bs = jnp.sum(jnp.where(pick, b_c, 0.0), axis=0, keepdims=True)
                    vs = jnp.sum(jnp.where(pick, v_c, 0.0), axis=0, keepdims=True)
                    dec = jnp.exp(jnp.minimum(b_c - bs, 0.0))
                    w = jnp.sum(q_c * ks * dec, axis=-1, keepdims=True)
                    w = jnp.where(t_idx >= s, w, 0.0)
                    return o_acc + w * vs

                o_intra = lax.fori_loop(0, C, s_body, jnp.zeros((C, HG_DV), F32))
                st = st_ref[hd]
                qe_c = (q_c * jnp.exp(b_c)).astype(BF16)
                o_inter = lax.dot_general(qe_c, st.astype(BF16), NT, preferred_element_type=F32)
                o_scr[pl.ds(r0, C), lo:lo + HG_DV] = o_intra + o_inter
                kl_c = (k_c * jnp.exp(bl - b_c)).astype(BF16)
                upd = lax.dot_general(v_c.astype(BF16), kl_c, TN, preferred_element_type=F32)
                st_ref[hd] = st * jnp.exp(bl) + upd
                return 0

            lax.fori_loop(0, nc, chunk_body, 0)

    o = o_scr[...]
    gate = g_scr[...]
    gout = gout_ref[...]
    for hd in range(HG_HEADS):
        lo = hd * HG_DV
        o_ref[:, lo:lo + HG_DV] = (_rms(o[:, lo:lo + HG_DV], gout) * gate[:, lo:lo + HG_DV]).astype(BF16)


def _mixer_in(x2, gmix, wcat, gcq, wuq, gckv, wukv, nmask, tabs, loglb, log1mlb, gout, tri, cmask,
              batch, seq, tm):
    nt = seq // tm
    n_sub = tm // HG_SUB
    n = batch * seq
    tab_spec = pl.BlockSpec((tm, LANES), lambda b, i: (i, 0))

    def row_spec(width):
        return pl.BlockSpec((tm, width), lambda b, i: (b * nt + i, 0))

    return pl.pallas_call(
        functools.partial(_mixer_in_kernel, tm=tm),
        grid=(batch, nt),
        in_specs=[
            row_spec(D_MODEL),
            _const_spec(gmix.shape), _const_spec(wcat.shape), _const_spec(gcq.shape),
            _const_spec(wuq.shape), _const_spec(gckv.shape), _const_spec(wukv.shape),
            _const_spec(nmask.shape), tab_spec, tab_spec, tab_spec,
            _const_spec(loglb.shape), _const_spec(log1mlb.shape), _const_spec(gout.shape),
            _const_spec(tri.shape), _const_spec(cmask.shape),
        ],
        out_specs=[
            pl.BlockSpec((None, n_sub, MLA_HEADS * HEAD_PAD, HG_SUB), lambda b, i: (b, i, 0, 0)),
            row_spec(MLA_HEADS * HEAD_PAD),
            pl.BlockSpec((None, MLA_HEADS // ATTN_GROUP, n_sub, ATTN_GROUP * MLA_V, HG_SUB),
                         lambda b, i: (b, 0, i, 0, 0)),
            row_spec(HG_WIDTH),
        ],
        out_shape=[
            jax.ShapeDtypeStruct((batch, seq // HG_SUB, MLA_HEADS * HEAD_PAD, HG_SUB), BF16),
            jax.ShapeDtypeStruct((n, MLA_HEADS * HEAD_PAD), BF16),
            jax.ShapeDtypeStruct((batch, MLA_HEADS // ATTN_GROUP, seq // HG_SUB, ATTN_GROUP * MLA_V, HG_SUB), BF16),
            jax.ShapeDtypeStruct((n, HG_WIDTH), BF16),
        ],
        scratch_shapes=[
            pltpu.VMEM((HG_HEADS, HG_DV, HG_DK), F32),
            pltpu.VMEM((HG_HEADS, HG_DV, HG_DK), F32),
            pltpu.VMEM((tm, HG_KW), F32),
            pltpu.VMEM((tm, HG_KW), F32),
            pltpu.VMEM((tm, HG_WIDTH), F32),
            pltpu.VMEM((tm, HG_KW), F32),
            pltpu.VMEM((tm, HG_WIDTH), F32),
            pltpu.VMEM((tm, HG_WIDTH), F32),
        ],
        compiler_params=_params(("parallel", "arbitrary")),
        name="mixer_in",
    )(x2, gmix, wcat, gcq, wuq, gckv, wukv, nmask, *tabs, loglb, log1mlb, gout, tri, cmask)


def _merge_kernel(x_ref, ya_ref, yb_ref, gmix_ref, wg_ref, wpa_ref, wpb_ref, wout_ref, o_ref):
    x = x_ref[...]
    h = _rms(x, gmix_ref[...]).astype(BF16)
    ya, yb = ya_ref[...], yb_ref[...]
    acc = x
    w = D_MODEL // MERGE_SPLIT
    for j in range(MERGE_SPLIT):
        cols = slice(j * w, (j + 1) * w)
        g_a = _dot(h, wg_ref[:, cols])
        g_b = _dot(h, wg_ref[:, D_MODEL + j * w:D_MODEL + (j + 1) * w])
        y = (jax.nn.sigmoid(g_a) * _dot(ya, wpa_ref[:, cols])
             + jax.nn.sigmoid(g_b) * _dot(yb, wpb_ref[:, cols]))
        acc = acc + _dot(y.astype(BF16), wout_ref[cols, :])
    o_ref[...] = acc


def _merge(x2, ya, yb, gmix, wg, wpa, wpb, wout, tm):
    n = x2.shape[0]
    return pl.pallas_call(
        _merge_kernel,
        grid=(n // tm,),
        in_specs=[
            pl.BlockSpec((tm, D_MODEL), lambda i: (i, 0)),
            pl.BlockSpec((tm, ya.shape[1]), lambda i: (i, 0)),
            pl.BlockSpec((tm, yb.shape[1]), lambda i: (i, 0)),
            _const_spec(gmix.shape), _const_spec(wg.shape), _const_spec(wpa.shape),
            _const_spec(wpb.shape), _const_spec(wout.shape),
        ],
        out_specs=pl.BlockSpec((tm, D_MODEL), lambda i: (i, 0)),
        out_shape=jax.ShapeDtypeStruct((n, D_MODEL), F32),
        compiler_params=_params(("parallel",)),
        name="merge",
    )(x2, ya, yb, gmix, wg, wpa, wpb, wout)


def _ffn_kernel(x_ref, gffn_ref, wgate_ref, wup_ref, wdown_ref, o_ref, *, n_split):
    x = x_ref[...]
    h = _rms(x, gffn_ref[...]).astype(BF16)
    d_ff = wgate_ref.shape[1]
    w = d_ff // n_split
    acc = x
    for j in range(n_split):
        gate = _dot(h, wgate_ref[:, j * w:(j + 1) * w])
        up = _dot(h, wup_ref[:, j * w:(j + 1) * w])
        act = (gate * jax.nn.sigmoid(gate) * up).astype(BF16)
        acc = acc + _dot(act, wdown_ref[j * w:(j + 1) * w, :])
    o_ref[...] = acc


def _ffn(x2, gffn, wgate, wup, wdown, tm, n_split):
    n = x2.shape[0]
    return pl.pallas_call(
        functools.partial(_ffn_kernel, n_split=n_split),
        grid=(n // tm,),
        in_specs=[
            pl.BlockSpec((tm, D_MODEL), lambda i: (i, 0)),
            _const_spec(gffn.shape), _const_spec(wgate.shape), _const_spec(wup.shape),
            _const_spec(wdown.shape),
        ],
        out_specs=pl.BlockSpec((tm, D_MODEL), lambda i: (i, 0)),
        out_shape=jax.ShapeDtypeStruct((n, D_MODEL), F32),
        compiler_params=_params(("parallel",)),
        name="ffn",
    )(x2, gffn, wgate, wup, wdown)


def _rope_tables(seq, gain, scale):
    pos = jnp.arange(seq, dtype=F32)
    inv_freq = 1.0 / (ROPE_THETA ** (jnp.arange(0, MLA_ROPE, 2, dtype=F32) / MLA_ROPE))
    ang = pos[:, None] * inv_freq[None, :]
    cos, sin = jnp.cos(ang), jnp.sin(ang)
    half = MLA_ROPE // 2
    g_nope, g_rope = gain[:MLA_NOPE], gain[MLA_NOPE:]
    nope = jnp.broadcast_to(g_nope[None, :], (seq, MLA_NOPE))
    c_rope = jnp.concatenate([cos, cos], axis=1) * g_rope[None, :]
    g_partner = jnp.concatenate([g_rope[half:], g_rope[:half]])
    s_rope = jnp.concatenate([-sin, sin], axis=1) * g_partner[None, :]
    zeros = jnp.zeros((seq, MLA_NOPE), F32)
    q_tab = jnp.concatenate([nope, c_rope, s_rope], axis=1) * scale
    ctab = jnp.concatenate([nope, c_rope, c_rope], axis=1) * scale
    stab = jnp.concatenate([zeros, s_rope, s_rope], axis=1) * scale
    return q_tab, ctab, stab


def _layer_params(l, seq, sub, w_in, mla_w_uq, mla_w_ukv, mla_q_norm, mla_k_norm, lower_bounds):
    o = [0]
    for s in IN_SPLITS:
        o.append(o[-1] + s)
    wi = w_in[l]
    half = MLA_ROPE // 2
    z64 = jnp.zeros((D_MODEL, MLA_NOPE), F32)
    w_kpe = wi[:, o[2]:o[3]]
    w_kpe_sw = jnp.concatenate([w_kpe[:, half:], w_kpe[:, :half]], axis=1)
    wmla = jnp.concatenate([wi[:, o[0]:o[2]], z64, w_kpe, w_kpe, z64, w_kpe_sw, w_kpe_sw], axis=1).astype(BF16)
    whg = wi[:, o[3]:o[7]].astype(BF16)
    wgates = wi[:, o[7]:o[9]].astype(BF16)

    uq = mla_w_uq[l].reshape(Q_LORA, MLA_HEADS, MLA_QK)
    uq_nope, uq_rope = uq[..., :MLA_NOPE], uq[..., MLA_NOPE:]
    uq_rope_sw = jnp.concatenate([uq_rope[..., half:], uq_rope[..., :half]], axis=-1)
    wuq = jnp.concatenate([uq_nope, uq_rope, uq_rope_sw], axis=-1).reshape(
        Q_LORA, MLA_HEADS * HEAD_PAD).astype(BF16)

    ukv = mla_w_ukv[l].reshape(KV_LORA, MLA_HEADS, MLA_NOPE + MLA_V)
    zk64 = jnp.zeros((KV_LORA, MLA_HEADS, HEAD_PAD - MLA_NOPE), F32)
    wuk = jnp.concatenate([ukv[..., :MLA_NOPE], zk64], axis=-1).reshape(KV_LORA, MLA_HEADS * HEAD_PAD)
    wuv = ukv[..., MLA_NOPE:].reshape(KV_LORA, MLA_HEADS * MLA_V)
    wukv = jnp.concatenate([wuk, wuv], axis=1).astype(BF16)

    tq_tab, _, _ = _rope_tables(seq, mla_q_norm[l], MLA_QK ** -0.5 * LOG2_E)
    _, ck, sk = _rope_tables(seq, mla_k_norm[l], 1.0)
    norm_mask = (jnp.arange(HEAD_PAD) < MLA_QK).astype(F32)[None, :]

    lb = lower_bounds[l][None, :]
    r = jnp.arange(sub)
    tri = ((r[:, None] // HG_CHUNK == r[None, :] // HG_CHUNK) & (r[:, None] >= r[None, :])).astype(BF16)
    nc = sub // HG_CHUNK
    cmask = (jnp.arange(nc * HG_DV)[:, None] // HG_DV == r[None, :] // HG_CHUNK).astype(BF16)
    return dict(wcat=jnp.concatenate([wmla, whg], axis=1), wgates=wgates, wuq=wuq, wukv=wukv,
                tabs=(tq_tab, ck, sk), norm_mask=norm_mask,
                loglb=jnp.log(lb), log1mlb=jnp.log1p(-lb), tri=tri, cmask=cmask)


def kernel(x, norm_mix, w_in, mla_norm_cq, mla_w_uq, mla_norm_ckv, mla_w_ukv, mla_q_norm, mla_k_norm,
           hg_lb_logits, hg_out_norm, w_proj_a, w_proj_b, w_out, norm_ffn, w_gate, w_up, w_down):
    batch, seq, d = x.shape
    assert d == D_MODEL
    depth = w_in.shape[0]
    tm = min(HG_SUB, seq)
    tm_hg = min(2 * HG_SUB, seq)
    tm_ffn = min(1024, seq)
    assert all(seq % t == 0 for t in (tm, tm_hg, tm_ffn))

    p = jax.nn.softmax(hg_lb_logits.astype(F32), axis=0)
    lower_bounds = jnp.maximum(jnp.cumsum(p, axis=0) - p[0:1], 0.0)

    x2 = x.reshape(batch * seq, d)
    for l in range(depth):
        lp = _layer_params(l, seq, min(HG_SUB, seq), w_in, mla_w_uq, mla_w_ukv, mla_q_norm, mla_k_norm, lower_bounds)
        gmix = norm_mix[l][None, :]
        qt, k, vt, y_b = _mixer_in(x2, gmix, lp["wcat"], mla_norm_cq[l][None, :], lp["wuq"],
                                   mla_norm_ckv[l][None, :], lp["wukv"], lp["norm_mask"], lp["tabs"],
                                   lp["loglb"], lp["log1mlb"], hg_out_norm[l][None, :], lp["tri"], lp["cmask"],
                                   batch, seq, tm_hg)
        y_a = _mla_attn(qt, k, vt, batch, seq, tm)
        x2 = _merge(x2, y_a, y_b, gmix, lp["wgates"], w_proj_a[l].astype(BF16),
                    w_proj_b[l].astype(BF16), w_out[l].astype(BF16), tm_ffn)
        x2 = _ffn(x2, norm_ffn[l][None, :], w_gate[l].astype(BF16), w_up[l].astype(BF16),
                  w_down[l].astype(BF16), tm_ffn, FFN_SPLIT)
    return x2.reshape(batch, seq, d)
```
